```python
import math
import jax, jax.numpy as jnp
from jax import lax
import numpy as np

D_MODEL = 1024
BATCH = 16
SEQ = 2048
DEPTH = 4
DEC_BATCH = 16
DEC_SEQ = 16
PAST_LEN = 1024

CHUNK = 64
Q_BLOCK = 128
MIX = D_MODEL
S5_WIDTH = MIX // 4
S5_GROUP = 16
S5_GROUPS = S5_WIDTH // S5_GROUP
S5_STATE = 64
GLA_WIDTH = MIX // 4
GLA_HEADS = 4
GLA_DV = GLA_WIDTH // GLA_HEADS
GLA_DK = GLA_DV // 2
GLA_KW = GLA_HEADS * GLA_DK
GLA_RANK = 16
GLA_TAU = 16.0
FOX_WIDTH = MIX // 2
FOX_HD = 64
FOX_HEADS = FOX_WIDTH // FOX_HD
FOX_F_BIAS_INIT = 3.0
D_FF = 4 * D_MODEL
EPS = 1e-6
IN_SIZES = (S5_WIDTH, GLA_KW, GLA_KW, GLA_WIDTH, GLA_WIDTH, GLA_RANK, FOX_WIDTH, FOX_WIDTH, FOX_WIDTH, FOX_HEADS)
N_IN = sum(IN_SIZES)
IN_SPLITS = tuple(int(s) for s in np.cumsum(IN_SIZES)[:-1])

kernel_name = 'hybrid_s5_gla_fox_stream_step'


def rmsnorm(x, g):
    xf = x.astype(jnp.float32)
    y = xf * lax.rsqrt(jnp.mean(xf * xf, axis=-1, keepdims=True) + EPS) * g.astype(jnp.float32)
    return y.astype(x.dtype)


def _s5_combine(e1, e2):
    a1r, a1i, b1r, b1i = e1
    a2r, a2i, b2r, b2i = e2
    return (a2r * a1r - a2i * a1i, a2r * a1i + a2i * a1r,
            a2r * b1r - a2i * b1i + b2r, a2r * b1i + a2i * b1r + b2i)


def s5_mixer(u, h0_re, h0_im, a_re, a_im, log_dt, b_re, b_im, c_re, c_im, d, glu_w, glu_b):
    f32 = jnp.float32
    bsz, L, _ = u.shape
    ug = u.astype(f32).reshape(bsz, L, S5_GROUPS, S5_GROUP)
    ar = a_re.astype(f32)
    ai = a_im.astype(f32)
    dt = jnp.exp(log_dt.astype(f32))[:, None]
    mag = jnp.exp(dt * ar)
    abar_re = mag * jnp.cos(dt * ai)
    abar_im = mag * jnp.sin(dt * ai)
    den = ar * ar + ai * ai
    zr = ((abar_re - 1.0) * ar + abar_im * ai) / den
    zi = (abar_im * ar - (abar_re - 1.0) * ai) / den
    br = b_re.astype(f32)
    bi = b_im.astype(f32)
    bb_re = zr[..., None] * br - zi[..., None] * bi
    bb_im = zr[..., None] * bi + zi[..., None] * br
    bu_re = jnp.einsum('blgp,gnp->blgn', ug, bb_re)
    bu_im = jnp.einsum('blgp,gnp->blgn', ug, bb_im)
    a_seq_re = jnp.broadcast_to(abar_re, (1, L, S5_GROUPS, S5_STATE))
    a_seq_im = jnp.broadcast_to(abar_im, (1, L, S5_GROUPS, S5_STATE))
    acum_re, acum_im, hz_re, hz_im = lax.associative_scan(
        _s5_combine, (a_seq_re, a_seq_im, bu_re, bu_im), axis=1)
    h0r = h0_re.astype(f32)[:, None]
    h0i = h0_im.astype(f32)[:, None]
    h_re = hz_re + acum_re * h0r - acum_im * h0i
    h_im = hz_im + acum_re * h0i + acum_im * h0r
    y = (jnp.einsum('gpn,blgn->blgp', c_re.astype(f32), h_re)
         - jnp.einsum('gpn,blgn->blgp', c_im.astype(f32), h_im)
         + d.astype(f32) * ug)
    y = jax.nn.gelu(y.reshape(bsz, L, S5_WIDTH))
    out = y * jax.nn.sigmoid(y @ glu_w.astype(f32) + glu_b.astype(f32))
    return out.astype(u.dtype), h_re[:, -1], h_im[:, -1]


def gla_mixer(q, k, v, g, a_lr, a2, a_bias, norm_g, S0):
    f32 = jnp.float32
    bsz, L, _ = q.shape
    C = min(CHUNK, L)
    n = L // C
    log_a = jax.nn.log_sigmoid((a_lr @ a2 + a_bias).astype(f32)) / GLA_TAU

    def heads(t, dh):
        return t.astype(f32).reshape(bsz, n, C, GLA_HEADS, dh).transpose(1, 0, 3, 2, 4)

    qh = heads(q, GLA_DK) * (GLA_DK ** -0.5)
    kh = heads(k, GLA_DK)
    vh = heads(v, GLA_DV)
    bh = jnp.cumsum(heads(log_a, GLA_DK), axis=3)
    causal = jnp.tril(jnp.ones((C, C), dtype=bool))

    def chunk_step(S, inp):
        qc, kc, vc, bc = inp
        q_dec = qc * jnp.exp(bc)
        k_inv = kc * jnp.exp(-bc)
        att = jnp.where(causal, jnp.einsum('bhtk,bhsk->bhts', q_dec, k_inv), 0.0)
        o = jnp.einsum('bhtk,bhkv->bhtv', q_dec, S) + jnp.einsum('bhts,bhsv->bhtv', att, vc)
        b_last = bc[:, :, -1]
        k_end = kc * jnp.exp(b_last[:, :, None, :] - bc)
        S_new = jnp.exp(b_last)[..., None] * S + jnp.einsum('bhsk,bhsv->bhkv', k_end, vc)
        return S_new, o

    S_fin, o = lax.scan(chunk_step, S0.astype(f32), (qh, kh, vh, bh))
    o = o.transpose(1, 0, 3, 2, 4).reshape(bsz, L, GLA_HEADS, GLA_DV)
    o = rmsnorm(o, norm_g).reshape(bsz, L, GLA_WIDTH)
    out = o * jax.nn.silu(g.astype(f32))
    return out.astype(q.dtype), S_fin


def fox_mixer(fq, fk, fv, f_logit, q_g, k_g, f_bias, past_k, past_v, past_logf):
    f32 = jnp.float32
    bsz, T, _ = fq.shape
    q = rmsnorm(fq.reshape(bsz, T, FOX_HEADS, FOX_HD), q_g)
    k = rmsnorm(fk.reshape(bsz, T, FOX_HEADS, FOX_HD), k_g)
    v = fv.reshape(bsz, T, FOX_HEADS, FOX_HD)
    logf = jax.nn.log_sigmoid(f_logit.astype(f32) + f_bias.astype(f32))
    if past_k is None:
        k_all, v_all, logf_all = k, v, logf
    else:
        k_all = jnp.concatenate([past_k.astype(k.dtype), k], axis=1)
        v_all = jnp.concatenate([past_v.astype(v.dtype), v], axis=1)
        logf_all = jnp.concatenate([past_logf.astype(f32), logf], axis=1)
    P = k_all.shape[1] - T
    c_all = jnp.cumsum(logf_all, axis=1).transpose(0, 2, 1)
    qh = q.astype(f32).transpose(0, 2, 1, 3)
    kh = k_all.astype(f32).transpose(0, 2, 1, 3)
    vh = v_all.astype(f32).transpose(0, 2, 1, 3)
    scale = FOX_HD ** -0.5
    outs = []
    for qs in range(0, T, Q_BLOCK):
        qe = min(T, qs + Q_BLOCK)
        ke = P + qe
        s = jnp.einsum('bhqd,bhkd->bhqk', qh[:, :, qs:qe], kh[:, :, :ke]) * scale
        s = s + c_all[:, :, P + qs:P + qe, None] - c_all[:, :, None, :ke]
        mask = jnp.arange(ke)[None, :] <= (P + jnp.arange(qs, qe))[:, None]
        p = jax.nn.softmax(jnp.where(mask, s, -jnp.inf), axis=-1)
        outs.append(jnp.einsum('bhqk,bhkd->bhqd', p, vh[:, :, :ke]))
    o = jnp.concatenate(outs, axis=2).transpose(0, 2, 1, 3).reshape(bsz, T, FOX_WIDTH)
    return o.astype(fq.dtype), k, v, logf


def trunk_layer(x, W, s5_h0_re, s5_h0_im, gla_S0, past_k, past_v, past_logf):
    h = rmsnorm(x, W['norm1_g'])
    u, gq, gk, gv, gg, ga, fq, fk, fv, ff = jnp.split(h @ W['w_in'], IN_SPLITS, axis=-1)
    o_s5, s5_re, s5_im = s5_mixer(u, s5_h0_re, s5_h0_im, W['s5_a_re'], W['s5_a_im'], W['s5_log_dt'],
                                  W['s5_b_re'], W['s5_b_im'], W['s5_c_re'], W['s5_c_im'], W['s5_d'],
                                  W['s5_glu_w'], W['s5_glu_b'])
    o_gla, gla_S = gla_mixer(gq, gk, gv, gg, ga, W['gla_a2'], W['gla_a_bias'], W['gla_norm_g'], gla_S0)
    o_fox, k_rows, v_rows, logf_rows = fox_mixer(fq, fk, fv, ff, W['fox_q_norm_g'], W['fox_k_norm_g'],
                                                 W['fox_f_bias'], past_k, past_v, past_logf)
    x = x + jnp.concatenate([o_s5, o_gla, o_fox], axis=-1) @ W['w_out']
    h2 = rmsnorm(x, W['norm2_g'])
    x = x + jnp.square(jax.nn.relu(h2 @ W['w_up'])) @ W['w_down']
    return x, (s5_re, s5_im, gla_S, k_rows, v_rows, logf_rows)


def setup_inputs(seed: int = 0) -> dict:
    key = jax.random.key(seed)
    ks = jax.random.split(key, 32)
    f32 = jnp.float32

    def nrm(k, shape, scale):
        return jax.random.normal(k, shape, f32) * scale

    n_idx = jnp.arange(S5_STATE, dtype=f32)
    return {
        'x_prompt': nrm(ks[0], (BATCH, SEQ, D_MODEL), 1.0),
        'x_sample': nrm(ks[1], (DEC_BATCH, DEC_SEQ, D_MODEL), 1.0),
        'state_s5_re': nrm(ks[2], (DEPTH, DEC_BATCH, S5_GROUPS, S5_STATE), 0.5),
        'state_s5_im': nrm(ks[3], (DEPTH, DEC_BATCH, S5_GROUPS, S5_STATE), 0.5),
        'state_gla': nrm(ks[4], (DEPTH, DEC_BATCH, GLA_HEADS, GLA_DK, GLA_DV), 1.0),
        'cache_fox_k': nrm(ks[5], (DEPTH, DEC_BATCH, PAST_LEN, FOX_HEADS, FOX_HD), 1.0),
        'cache_fox_v': nrm(ks[6], (DEPTH, DEC_BATCH, PAST_LEN, FOX_HEADS, FOX_HD), 1.0),
        'cache_fox_logf': jax.nn.log_sigmoid(FOX_F_BIAS_INIT + nrm(ks[7], (DEPTH, DEC_BATCH, PAST_LEN, FOX_HEADS), 1.0)),
        'norm1_g': 1.0 + nrm(ks[8], (DEPTH, D_MODEL), 0.02),
        'w_in': nrm(ks[9], (DEPTH, D_MODEL, N_IN), D_MODEL ** -0.5),
        's5_a_re': -0.5 + nrm(ks[10], (DEPTH, S5_GROUPS, S5_STATE), 0.01),
        's5_a_im': math.pi * n_idx + nrm(ks[11], (DEPTH, S5_GROUPS, S5_STATE), 0.01),
        's5_log_dt': math.log(0.001) + jax.random.uniform(ks[12], (DEPTH, S5_GROUPS), f32) * (math.log(0.1) - math.log(0.001)),
        's5_b_re': nrm(ks[13], (DEPTH, S5_GROUPS, S5_STATE, S5_GROUP), (2 * S5_GROUP) ** -0.5),
        's5_b_im': nrm(ks[14], (DEPTH, S5_GROUPS, S5_STATE, S5_GROUP), (2 * S5_GROUP) ** -0.5),
        's5_c_re': nrm(ks[15], (DEPTH, S5_GROUPS, S5_GROUP, S5_STATE), (2 * S5_STATE) ** -0.5),
        's5_c_im': nrm(ks[16], (DEPTH, S5_GROUPS, S5_GROUP, S5_STATE), (2 * S5_STATE) ** -0.5),
        's5_d': nrm(ks[17], (DEPTH, S5_GROUPS, S5_GROUP), 1.0),
        's5_glu_w': nrm(ks[18], (DEPTH, S5_WIDTH, S5_WIDTH), S5_WIDTH ** -0.5),
        's5_glu_b': nrm(ks[19], (DEPTH, S5_WIDTH), 0.01),
        'gla_a2': nrm(ks[20], (DEPTH, GLA_RANK, GLA_KW), GLA_RANK ** -0.5),
        'gla_a_bias': nrm(ks[21], (DEPTH, GLA_KW), 0.01),
        'gla_norm_g': 1.0 + nrm(ks[22], (DEPTH, GLA_DV), 0.02),
        'fox_q_norm_g': 1.0 + nrm(ks[23], (DEPTH, FOX_HD), 0.02),
        'fox_k_norm_g': 1.0 + nrm(ks[24], (DEPTH, FOX_HD), 0.02),
        'fox_f_bias': FOX_F_BIAS_INIT + nrm(ks[25], (DEPTH, FOX_HEADS), 0.1),
        'w_out': nrm(ks[26], (DEPTH, MIX, D_MODEL), MIX ** -0.5),
        'norm2_g': 1.0 + nrm(ks[27], (DEPTH, D_MODEL), 0.02),
        'w_up': nrm(ks[28], (DEPTH, D_MODEL, D_FF), D_MODEL ** -0.5),
        'w_down': nrm(ks[29], (DEPTH, D_FF, D_MODEL), D_FF ** -0.5),
    }


def reference(x_prompt, x_sample, state_s5_re, state_s5_im, state_gla, cache_fox_k, cache_fox_v, cache_fox_logf,
              norm1_g, w_in, s5_a_re, s5_a_im, s5_log_dt, s5_b_re, s5_b_im, s5_c_re, s5_c_im, s5_d,
              s5_glu_w, s5_glu_b, gla_a2, gla_a_bias, gla_norm_g, fox_q_norm_g, fox_k_norm_g, fox_f_bias,
              w_out, norm2_g, w_up, w_down):
    yp, ys = x_prompt, x_sample
    bp = x_prompt.shape[0]
    p_st = [[] for _ in range(6)]
    s_st = [[] for _ in range(6)]
    for l in range(DEPTH):
        W = {'norm1_g': norm1_g[l], 'w_in': w_in[l], 's5_a_re': s5_a_re[l], 's5_a_im': s5_a_im[l],
             's5_log_dt': s5_log_dt[l], 's5_b_re': s5_b_re[l], 's5_b_im': s5_b_im[l], 's5_c_re': s5_c_re[l],
             's5_c_im': s5_c_im[l], 's5_d': s5_d[l], 's5_glu_w': s5_glu_w[l], 's5_glu_b': s5_glu_b[l],
             'gla_a2': gla_a2[l], 'gla_a_bias': gla_a_bias[l], 'gla_norm_g': gla_norm_g[l],
             'fox_q_norm_g': fox_q_norm_g[l], 'fox_k_norm_g': fox_k_norm_g[l], 'fox_f_bias': fox_f_bias[l],
             'w_out': w_out[l], 'norm2_g': norm2_g[l], 'w_up': w_up[l], 'w_down': w_down[l]}
        z_s5 = jnp.zeros((bp, S5_GROUPS, S5_STATE), jnp.float32)
        z_gla = jnp.zeros((bp, GLA_HEADS, GLA_DK, GLA_DV), jnp.float32)
        yp, stp = trunk_layer(yp, W, z_s5, z_s5, z_gla, None, None, None)
        ys, sts = trunk_layer(ys, W, state_s5_re[l], state_s5_im[l], state_gla[l],
                              cache_fox_k[l], cache_fox_v[l], cache_fox_logf[l])
        for i in range(6):
            p_st[i].append(stp[i])
            s_st[i].append(sts[i])
    p_s5_re, p_s5_im, p_gla, p_fox_k, p_fox_v, p_fox_logf = [jnp.stack(a) for a in p_st]
    s_s5_re, s_s5_im, s_gla, s_fox_k, s_fox_v, s_fox_logf = [jnp.stack(a) for a in s_st]
    return (yp, ys, p_s5_re, p_s5_im, p_gla, p_fox_k, p_fox_v, p_fox_logf,
            s_s5_re, s_s5_im, s_gla, s_fox_k, s_fox_v, s_fox_logf)
```

```python
import functools
import math

import jax
import jax.numpy as jnp
import numpy as np
from jax import lax
from jax.experimental import pallas as pl
from jax.experimental.pallas import tpu as pltpu

F32 = jnp.float32
BF16 = jnp.bfloat16

D_MODEL = 1024
EPS = 1e-6
LANES = 128

S5_WIDTH = 256
S5_GROUPS = 16
S5_GROUP = 16
S5_STATE = 64
S5_GN = S5_GROUPS * S5_STATE
S5_LANE_CHUNK = 256

GLA_WIDTH = 256
GLA_HEADS = 4
GLA_DK = 32
GLA_DV = 64
GLA_KW = 128
GLA_RANK = 16
GLA_TAU = 16.0
GLA_CHUNK = 64

FOX_WIDTH = 512
FOX_HD = 64
FOX_HEADS = 8
FOX_PAIRS = FOX_WIDTH // LANES
NEG_BIG = -1e30

D_FF = 4096
FF_CHUNK = 1024

COL_U = 0
COL_QK = 256
COL_GV = 512
COL_GG = 768
COL_MISC = 1024
COL_FQ = 1152
COL_FK = 1664
COL_FV = 2176
N_PACK = 2688
MISC_FF = 0
MISC_GA = 8

VMEM_LIMIT = 56 * 1024 * 1024


def _cparams(sem):
    return pltpu.CompilerParams(dimension_semantics=sem, vmem_limit_bytes=VMEM_LIMIT)


def _const_spec(shape):
    nd = len(shape)
    return pl.BlockSpec(shape, lambda *_: (0,) * nd)


def _log_sigmoid(x):
    return jnp.minimum(x, 0.0) - jnp.log1p(jnp.exp(-jnp.abs(x)))


def _sigmoid(x):
    return 1.0 / (1.0 + jnp.exp(-x))


def _split2(x):
    hi = x.astype(BF16)
    lo = (x - hi.astype(F32)).astype(BF16)
    return hi, lo


def _split3(x):
    hi = x.astype(BF16)
    r = x - hi.astype(F32)
    mid = r.astype(BF16)
    lo = (r - mid.astype(F32)).astype(BF16)
    return hi, mid, lo


def _dot(a, b):
    return jnp.dot(a, b, preferred_element_type=F32)


def _dot_nt(a, b):
    return lax.dot_general(a, b, (((1,), (1,)), ((), ())), preferred_element_type=F32)


def _dot_tn(a, b):
    return lax.dot_general(a, b, (((0,), (0,)), ((), ())), preferred_element_type=F32)


def _segment_rms_scale(x, seg_ref, width):
    hi, lo = _split2(x * x)
    seg = seg_ref[...]
    ms = (_dot(hi, seg) + _dot(lo, seg)) * (1.0 / width)
    return lax.rsqrt(ms + EPS)


def _in_proj_kernel(x_ref, g_ref, w_ref, wft_ref, fb_row_ref, fb_col_ref,
                    u_ref, qk_ref, gv_ref, gg_ref, misc_ref, fq_ref, fk_ref, fv_ref, logf_ref, logft_ref):
    x = x_ref[...]
    ms = jnp.mean(x * x, axis=-1, keepdims=True)
    h = (x * lax.rsqrt(ms + EPS) * g_ref[...]).astype(BF16)
    for ref, c0 in ((u_ref, COL_U), (qk_ref, COL_QK), (gv_ref, COL_GV), (gg_ref, COL_GG),
                    (fq_ref, COL_FQ), (fk_ref, COL_FK), (fv_ref, COL_FV)):
        ref[...] = _dot(h, w_ref[:, c0:c0 + ref.shape[1]])
    misc = _dot(h, w_ref[:, COL_MISC:COL_MISC + LANES])
    misc_ref[...] = misc
    logf_ref[...] = _log_sigmoid(misc[:, MISC_FF:MISC_FF + FOX_HEADS] + fb_row_ref[...])
    ft = _dot_nt(wft_ref[...], h)
    logft_ref[...] = _log_sigmoid(ft[:FOX_HEADS, :] + fb_col_ref[...])


def _in_proj(x2d, g, w_pack, wft, fb_row, fb_col, tm):
    n = x2d.shape[0]
    row = lambda w: pl.BlockSpec((tm, w), lambda i: (i, 0))
    widths = (S5_WIDTH, 256, GLA_WIDTH, GLA_WIDTH, LANES, FOX_WIDTH, FOX_WIDTH, FOX_WIDTH, FOX_HEADS)
    out_shape = [jax.ShapeDtypeStruct((n, w), F32) for w in widths] + [jax.ShapeDtypeStruct((FOX_HEADS, n), F32)]
    out_specs = [row(w) for w in widths] + [pl.BlockSpec((FOX_HEADS, tm), lambda i: (0, i))]
    return pl.pallas_call(
        _in_proj_kernel,
        grid=(n // tm,),
        in_specs=[row(D_MODEL), _const_spec((1, D_MODEL)), _const_spec((D_MODEL, N_PACK)),
                  _const_spec((16, D_MODEL)), _const_spec((1, FOX_HEADS)), _const_spec((FOX_HEADS, 1))],
        out_specs=out_specs,
        out_shape=out_shape,
        compiler_params=_cparams(("parallel",)),
        name="in_proj",
    )(x2d, g, w_pack, wft, fb_row, fb_col)


def _gelu_tanh(x):
    c = math.sqrt(2.0 / math.pi)
    return x * (0.5 * (1.0 + jnp.tanh(c * (x + 0.044715 * (x * x * x)))))


def _s5_kernel(u_ref, h0r_ref, h0i_ref, ar_ref, ai_ref, bmat_ref, cmat_ref, d_ref, gw_ref, gb_ref,
               o_ref, hr_out_ref, hi_out_ref, hbuf, hr_s, hi_s, *, tl, nb):
    i = pl.program_id(0)

    @pl.when(i == 0)
    def _():
        hr_s[...] = h0r_ref[...]
        hi_s[...] = h0i_ref[...]

    rows = tl * nb
    u = u_ref[...].reshape(rows, S5_WIDTH)
    hbuf[...] = _dot(u.astype(BF16), bmat_ref[...])

    for c in range(S5_GN // S5_LANE_CHUNK):
        lo = c * S5_LANE_CHUNK
        re = slice(lo, lo + S5_LANE_CHUNK)
        im = slice(S5_GN + lo, S5_GN + lo + S5_LANE_CHUNK)
        ar = jnp.broadcast_to(ar_ref[:, re], (nb, S5_LANE_CHUNK))
        ai = jnp.broadcast_to(ai_ref[:, re], (nb, S5_LANE_CHUNK))

        def body(t, carry, re=re, im=im, ar=ar, ai=ai):
            hr, hi = carry
            r = pl.ds(pl.multiple_of(t * nb, nb), nb)
            nhr = ar * hr - ai * hi + hbuf[r, re]
            nhi = ar * hi + ai * hr + hbuf[r, im]
            hbuf[r, re] = nhr
            hbuf[r, im] = nhi
            return nhr, nhi

        hr, hi = lax.fori_loop(0, tl, body, (hr_s[:, re], hi_s[:, re]), unroll=min(tl, 8))
        hr_s[:, re] = hr
        hi_s[:, re] = hi

    y = _dot(hbuf[...].astype(BF16), cmat_ref[...]) + d_ref[...] * u
    y = _gelu_tanh(y)
    z = _dot(y.astype(BF16), gw_ref[...]) + gb_ref[...]
    o_ref[...] = (y * _sigmoid(z)).reshape(tl, nb, S5_WIDTH)
    hr_out_ref[...] = hr_s[...]
    hi_out_ref[...] = hi_s[...]


def _s5(u_t, h0r, h0i, ar, ai, bmat, cmat, d, gw, gb, tl):
    seq, nb, _ = u_t.shape
    kern = functools.partial(_s5_kernel, tl=tl, nb=nb)
    state = _const_spec((nb, S5_GN))
    return pl.pallas_call(
        kern,
        grid=(seq // tl,),
        in_specs=[pl.BlockSpec((tl, nb, S5_WIDTH), lambda i: (i, 0, 0)), state, state,
                  _const_spec((1, S5_GN)), _const_spec((1, S5_GN)),
                  _const_spec((S5_WIDTH, 2 * S5_GN)), _const_spec((2 * S5_GN, S5_WIDTH)),
                  _const_spec((1, S5_WIDTH)), _const_spec((S5_WIDTH, S5_WIDTH)), _const_spec((1, S5_WIDTH))],
        out_specs=[pl.BlockSpec((tl, nb, S5_WIDTH), lambda i: (i, 0, 0)), state, state],
        out_shape=[jax.ShapeDtypeStruct((seq, nb, S5_WIDTH), F32),
                   jax.ShapeDtypeStruct((nb, S5_GN), F32), jax.ShapeDtypeStruct((nb, S5_GN), F32)],
        scratch_shapes=[pltpu.VMEM((tl * nb, 2 * S5_GN), F32),
                        pltpu.VMEM((nb, S5_GN), F32), pltpu.VMEM((nb, S5_GN), F32)],
        compiler_params=_cparams(("arbitrary",)),
        name="s5",
    )(u_t, h0r, h0i, ar, ai, bmat, cmat, d, gw, gb)


def _gla_kernel(qk_ref, v_ref, g_ref, misc_ref, a2_ref, ab_ref, ng_ref, s0_ref,
                tril_ref, mk_ref, mv_ref, ms_ref, causal_ref, seg_ref,
                o_ref, sfin_ref, s_scr, *, tl, chunk):
    i = pl.program_id(1)

    @pl.when(i == 0)
    def _():
        s_scr[...] = s0_ref[0]

    qk = qk_ref[0]
    q = qk[:, :GLA_KW] * (GLA_DK ** -0.5)
    k = qk[:, GLA_KW:]
    v = v_ref[0]
    la = _log_sigmoid(_dot(misc_ref[0].astype(BF16), a2_ref[...]) + ab_ref[...]) * (1.0 / GLA_TAU)

    tril = tril_ref[...]
    b = sum(_dot(tril, part) for part in _split3(la))
    nchunk = tl // chunk
    b3 = b.reshape(nchunk, chunk, GLA_KW)
    b_last = b3[:, chunk - 1:chunk, :]
    b_end = (b_last - b3).reshape(tl, GLA_KW)
    q_dec = (q * jnp.exp(b)).astype(BF16)
    k_inv = k * jnp.exp(-b)
    k_end = (k * jnp.exp(b_end)).astype(BF16)
    e_last = jnp.exp(b_last)

    outs = []
    for c in range(nchunk):
        sl = slice(c * chunk, (c + 1) * chunk)
        qd = q_dec[sl]
        kbd = jnp.where(mk_ref[...] > 0, jnp.concatenate([k_inv[sl]] * GLA_HEADS, axis=0), 0.0).astype(BF16)
        att = jnp.where(causal_ref[...] > 0, _dot_nt(qd, kbd), 0.0)
        vc = v[sl]
        vbd = jnp.where(mv_ref[...] > 0, jnp.concatenate([vc] * GLA_HEADS, axis=0), 0.0).astype(BF16)
        s = s_scr[...]
        outs.append(_dot(att.astype(BF16), vbd) + _dot_nt(qd, s.astype(BF16)))
        upd = jnp.where(ms_ref[...] > 0, _dot_tn(vc.astype(BF16), k_end[sl]), 0.0)
        s_scr[...] = s * e_last[c] + upd
    o = outs[0] if nchunk == 1 else jnp.concatenate(outs, axis=0)

    o = o * _segment_rms_scale(o, seg_ref, GLA_DV) * ng_ref[...]
    g = g_ref[0]
    o_ref[0] = o * (g * _sigmoid(g))
    sfin_ref[0] = s_scr[...]


def _gla_masks(tl, chunk):
    t = np.arange(tl)
    tril = ((t[:, None] // chunk == t[None, :] // chunk) & (t[None, :] <= t[:, None])).astype(np.float32)
    r = np.arange(GLA_HEADS * chunk)
    mk = (r[:, None] // chunk == np.arange(GLA_KW)[None, :] // GLA_DK).astype(np.float32)
    mv = (r[:, None] // chunk == np.arange(GLA_WIDTH)[None, :] // GLA_DV).astype(np.float32)
    ms = (np.arange(GLA_WIDTH)[:, None] // GLA_DV == np.arange(GLA_KW)[None, :] // GLA_DK).astype(np.float32)
    causal = (np.arange(chunk)[:, None] >= (r[None, :] % chunk)).astype(np.float32)
    seg = (np.arange(GLA_WIDTH)[:, None] // GLA_DV == np.arange(GLA_WIDTH)[None, :] // GLA_DV).astype(np.float32)
    return (jnp.asarray(tril, BF16), jnp.asarray(mk), jnp.asarray(mv), jnp.asarray(ms), jnp.asarray(causal),
            jnp.asarray(seg, BF16))


def _gla(qk, v, g, misc, a2p, ab, ng, s0t, tl, chunk):
    nb, seq, _ = qk.shape
    kern = functools.partial(_gla_kernel, tl=tl, chunk=chunk)
    masks = _gla_masks(tl, chunk)
    tok = lambda w: pl.BlockSpec((1, tl, w), lambda b, i: (b, i, 0))
    st = pl.BlockSpec((1, GLA_WIDTH, GLA_KW), lambda b, i: (b, 0, 0))
    return pl.pallas_call(
        kern,
        grid=(nb, seq // tl),
        in_specs=[tok(256), tok(GLA_WIDTH), tok(GLA_WIDTH), tok(LANES),
                  _const_spec((LANES, GLA_KW)), _const_spec((1, GLA_KW)), _const_spec((1, GLA_WIDTH)), st]
                 + [_const_spec(m.shape) for m in masks],
        out_specs=[tok(GLA_WIDTH), st],
        out_shape=[jax.ShapeDtypeStruct((nb, seq, GLA_WIDTH), F32),
                   jax.ShapeDtypeStruct((nb, GLA_WIDTH, GLA_KW), F32)],
        scratch_shapes=[pltpu.VMEM((GLA_WIDTH, GLA_KW), F32)],
        compiler_params=_cparams(("parallel", "arbitrary")),
        name="gla",
    )(qk, v, g, misc, a2p, ab, ng, s0t, *masks)


def _lane_cumsum(x, carry, ut):
    x16 = jnp.concatenate([x, jnp.zeros_like(x)], axis=0)
    c = sum(_dot(part, ut) for part in _split3(x16))[:x.shape[0]] + carry
    return c, c[:, x.shape[1] - 1:]


def _fox_kernel(*refs, tq, tk, seq, past, tkp):
    if past:
        (fq_ref, fk_ref, fv_ref, lft_ref, qg_ref, kg_ref, seg_ref, ut_ref, pk_ref, pv_ref, plft_ref, utp_ref,
         o_ref, krow_ref, kb, vb, nck, pkb, pvb, pnck) = refs
    else:
        (fq_ref, fk_ref, fv_ref, lft_ref, qg_ref, kg_ref, seg_ref, ut_ref,
         o_ref, krow_ref, kb, vb, nck) = refs
    i = pl.program_id(1)
    blk = min(tk, seq)

    @pl.when(i == 0)
    def _prologue():
        carry = jnp.zeros((FOX_HEADS, 1), F32)
        if past:
            for jb in range(past // tkp):
                r = slice(jb * tkp, (jb + 1) * tkp)
                pkb[r, :] = pk_ref[0, r, :].astype(BF16)
                pvb[r, :] = pv_ref[0, r, :].astype(BF16)
                c, carry = _lane_cumsum(plft_ref[0, :, r], carry, utp_ref[...])
                pnck[jb] = -c
        if blk < tk:
            kb[...] = jnp.zeros_like(kb)
            vb[...] = jnp.zeros_like(vb)
        for jb in range(max(seq // tk, 1)):
            r = slice(jb * tk, jb * tk + blk)
            kf = fk_ref[0, r, :]
            kn = kf * _segment_rms_scale(kf, seg_ref, FOX_HD) * kg_ref[...]
            krow_ref[0, r, :] = kn
            kb[r, :] = kn.astype(BF16)
            vb[r, :] = fv_ref[0, r, :].astype(BF16)
            c, carry = _lane_cumsum(lft_ref[0, :, jb * tk:(jb + 1) * tk], carry, ut_ref[...])
            nck[jb] = -c

    qf = fq_ref[0]
    qn = qf * _segment_rms_scale(qf, seg_ref, FOX_HD) * qg_ref[...] * (FOX_HD ** -0.5)
    rowi = lax.broadcasted_iota(jnp.int32, (tq, tk), 0)
    coli = lax.broadcasted_iota(jnp.int32, (tq, tk), 1)
    diag_ok = coli <= rowi
    low_half = lax.broadcasted_iota(jnp.int32, (1, LANES), 1) < FOX_HD

    def step(carry, qm, kblk, vblk, bias, mask):
        m, l, acc = carry
        s = _dot_nt(qm, kblk) + bias
        if mask is not None:
            s = jnp.where(mask, s, NEG_BIG)
        m_new = jnp.maximum(m, jnp.max(s, axis=-1, keepdims=True))
        p = jnp.exp(s - m_new)
        alpha = jnp.exp(m - m_new)
        l = alpha * l + jnp.sum(p, axis=-1, keepdims=True)
        acc = alpha * acc + _dot(p.astype(BF16), vblk)
        return m_new, l, acc

    for pair in range(FOX_PAIRS):
        lanes = slice(pair * LANES, (pair + 1) * LANES)
        qp = qn[:, lanes]
        halves = []
        for hh in range(2):
            head = 2 * pair + hh
            hrow = pl.ds(head, 1)
            qm = jnp.where(low_half if hh == 0 else jnp.logical_not(low_half), qp, 0.0).astype(BF16)
            carry = (jnp.full((tq, 1), NEG_BIG, F32), jnp.zeros((tq, 1), F32), jnp.zeros((tq, LANES), F32))
            if past:
                for jb in range(past // tkp):
                    r = slice(jb * tkp, (jb + 1) * tkp)
                    carry = step(carry, qm, pkb[r, lanes], pvb[r, lanes], pnck[jb, hrow, :], None)

            def body(j, carry, qm=qm, lanes=lanes, hrow=hrow):
                r = pl.ds(pl.multiple_of(j * tk, tk), tk)
                return step(carry, qm, kb[r, lanes], vb[r, lanes], nck[j, hrow, :], None)

            carry = lax.fori_loop(0, i, body, carry)
            r = pl.ds(pl.multiple_of(i * tk, tk), tk)
            _, l, acc = step(carry, qm, kb[r, lanes], vb[r, lanes], nck[i, hrow, :], diag_ok)
            halves.append(acc / l)
        o_ref[0, :, lanes] = jnp.where(low_half, halves[0], halves[1])


def _upper_ones(n):
    t = np.arange(n)
    return jnp.asarray((t[:, None] <= t[None, :]).astype(np.float32), BF16)


def _fox(fq, fk, fv, lft, qg, kg, tq, tk, past_k=None, past_v=None, past_lft=None):
    nb, seq, _ = fq.shape
    nq = seq // tq
    assert tq == tk or nq == 1
    nkb = max(seq // tk, 1)
    past = 0 if past_k is None else past_k.shape[1]
    tkp = min(past, 256) if past else 0
    t = np.arange(FOX_WIDTH)
    seg = jnp.asarray((t[:, None] // FOX_HD == t[None, :] // FOX_HD).astype(np.float32), BF16)
    kern = functools.partial(_fox_kernel, tq=tq, tk=tk, seq=seq, past=past, tkp=tkp)
    full = lambda rows: pl.BlockSpec((1, rows, FOX_WIDTH), lambda b, i: (b, 0, 0))
    in_specs = [pl.BlockSpec((1, tq, FOX_WIDTH), lambda b, i: (b, i, 0)), full(seq), full(seq),
                pl.BlockSpec((1, FOX_HEADS, nkb * tk), lambda b, i: (b, 0, 0)),
                _const_spec((1, FOX_WIDTH)), _const_spec((1, FOX_WIDTH)),
                _const_spec((FOX_WIDTH, FOX_WIDTH)), _const_spec((tk, tk))]
    args = [fq, fk, fv, lft, qg, kg, seg, _upper_ones(tk)]
    scratch = [pltpu.VMEM((nkb * tk, FOX_WIDTH), BF16), pltpu.VMEM((nkb * tk, FOX_WIDTH), BF16),
               pltpu.VMEM((nkb, FOX_HEADS, tk), F32)]
    if past:
        in_specs += [full(past), full(past), pl.BlockSpec((1, FOX_HEADS, past), lambda b, i: (b, 0, 0)),
                     _const_spec((tkp, tkp))]
        args += [past_k, past_v, past_lft, _upper_ones(tkp)]
        scratch += [pltpu.VMEM((past, FOX_WIDTH), BF16), pltpu.VMEM((past, FOX_WIDTH), BF16),
                    pltpu.VMEM((past // tkp, FOX_HEADS, tkp), F32)]
    return pl.pallas_call(
        kern,
        grid=(nb, nq),
        in_specs=in_specs,
        out_specs=[pl.BlockSpec((1, tq, FOX_WIDTH), lambda b, i: (b, i, 0)), full(seq)],
        out_shape=[jax.ShapeDtypeStruct((nb, seq, FOX_WIDTH), F32), jax.ShapeDtypeStruct((nb, seq, FOX_WIDTH), F32)],
        scratch_shapes=scratch,
        compiler_params=_cparams(("parallel", "arbitrary")),
        name="fox",
    )(*args)


def _out_mlp_kernel(x_ref, s5_ref, gla_ref, fox_ref, wo_ref, g2_ref, wu_ref, wd_ref, y_ref, up_s):
    x1 = (x_ref[...]
          + _dot(s5_ref[...].astype(BF16), wo_ref[0:S5_WIDTH, :])
          + _dot(gla_ref[...].astype(BF16), wo_ref[S5_WIDTH:S5_WIDTH + GLA_WIDTH, :])
          + _dot(fox_ref[...].astype(BF16), wo_ref[S5_WIDTH + GLA_WIDTH:, :]))
    ms = jnp.mean(x1 * x1, axis=-1, keepdims=True)
    h2 = (x1 * lax.rsqrt(ms + EPS) * g2_ref[...]).astype(BF16)
    for c in range(D_FF // FF_CHUNK):
        cols = slice(c * FF_CHUNK, (c + 1) * FF_CHUNK)
        up = jnp.maximum(_dot(h2, wu_ref[:, cols]), 0.0)
        up_s[:, cols] = (up * up).astype(BF16)
    y_ref[...] = x1 + _dot(up_s[...], wd_ref[...])


def _out_mlp(x2d, s5, gla, fox, wo, g2, wu, wd, tm):
    n = x2d.shape[0]
    row = lambda w: pl.BlockSpec((tm, w), lambda i: (i, 0))
    resident = pl.BlockSpec(memory_space=pltpu.VMEM)
    return pl.pallas_call(
        _out_mlp_kernel,
        grid=(n // tm,),
        in_specs=[row(D_MODEL), row(S5_WIDTH), row(GLA_WIDTH), row(FOX_WIDTH),
                  resident, _const_spec((1, D_MODEL)), resident, resident],
        out_specs=row(D_MODEL),
        out_shape=jax.ShapeDtypeStruct((n, D_MODEL), F32),
        scratch_shapes=[pltpu.VMEM((tm, D_FF), BF16)],
        compiler_params=_cparams(("parallel",)),
        name="out_mlp",
    )(x2d, s5, gla, fox, wo, g2, wu, wd)


def _pack_params(norm1_g, w_in, s5_a_re, s5_a_im, s5_log_dt, s5_b_re, s5_b_im, s5_c_re, s5_c_im, s5_d,
                 s5_glu_w, s5_glu_b, gla_a2, gla_a_bias, gla_norm_g, fox_q_norm_g, fox_k_norm_g, fox_f_bias,
                 w_out, norm2_g, w_up, w_down):
    depth = w_in.shape[0]
    o_ga = 1024
    o_fq = o_ga + GLA_RANK
    o_ff = o_fq + 3 * FOX_WIDTH
    w_ff = w_in[:, :, o_ff:o_ff + FOX_HEADS]
    pad = jnp.zeros((depth, D_MODEL, LANES - FOX_HEADS - GLA_RANK), F32)
    w_pack = jnp.concatenate([w_in[:, :, :o_ga], w_ff, w_in[:, :, o_ga:o_fq], pad, w_in[:, :, o_fq:o_ff]],
                             axis=-1).astype(BF16)
    wft = jnp.pad(jnp.swapaxes(w_ff, 1, 2), ((0, 0), (0, 16 - FOX_HEADS), (0, 0))).astype(BF16)
    a2p = jnp.zeros((depth, LANES, GLA_KW), F32).at[:, MISC_GA:MISC_GA + GLA_RANK, :].set(gla_a2).astype(BF16)

    dt = jnp.exp(s5_log_dt)[..., None]
    mag = jnp.exp(dt * s5_a_re)
    abar_re = mag * jnp.cos(dt * s5_a_im)
    abar_im = mag * jnp.sin(dt * s5_a_im)
    den = s5_a_re * s5_a_re + s5_a_im * s5_a_im
    zr = ((abar_re - 1.0) * s5_a_re + abar_im * s5_a_im) / den
    zi = (abar_im * s5_a_re - (abar_re - 1.0) * s5_a_im) / den
    bb_re = zr[..., None] * s5_b_re - zi[..., None] * s5_b_im
    bb_im = zr[..., None] * s5_b_im + zi[..., None] * s5_b_re
    eye = jnp.eye(S5_GROUPS, dtype=F32)
    to_b = lambda bb: jnp.einsum('lgnp,gh->lgphn', bb, eye).reshape(depth, S5_WIDTH, S5_GN)
    bmat = jnp.concatenate([to_b(bb_re), to_b(bb_im)], axis=-1).astype(BF16)
    to_c = lambda cc: jnp.einsum('lgpn,gh->lgnhp', cc, eye).reshape(depth, S5_GN, S5_WIDTH)
    cmat = jnp.concatenate([to_c(s5_c_re), -to_c(s5_c_im)], axis=1).astype(BF16)

    return dict(
        norm1_g=norm1_g[:, None, :], w_pack=w_pack, wft=wft,
        fb_row=fox_f_bias[:, None, :], fb_col=fox_f_bias[:, :, None],
        abar_re=abar_re.reshape(depth, 1, S5_GN), abar_im=abar_im.reshape(depth, 1, S5_GN),
        bmat=bmat, cmat=cmat, s5_d=s5_d.reshape(depth, 1, S5_WIDTH),
        glu_w=s5_glu_w.astype(BF16), glu_b=s5_glu_b[:, None, :],
        a2p=a2p, ab=gla_a_bias[:, None, :], ng=jnp.tile(gla_norm_g, (1, GLA_HEADS))[:, None, :],
        qg=jnp.tile(fox_q_norm_g, (1, FOX_HEADS))[:, None, :], kg=jnp.tile(fox_k_norm_g, (1, FOX_HEADS))[:, None, :],
        w_out=w_out.astype(BF16), norm2_g=norm2_g[:, None, :], w_up=w_up.astype(BF16), w_down=w_down.astype(BF16),
    )


def _gla_state_to_kernel(s):
    eye = jnp.eye(GLA_HEADS, dtype=s.dtype)
    return jnp.einsum('bhkv,hg->bhvgk', s, eye).reshape(s.shape[0], GLA_WIDTH, GLA_KW)


def _gla_state_from_kernel(st):
    nb = st.shape[0]
    s5d = st.reshape(nb, GLA_HEADS, GLA_DV, GLA_HEADS, GLA_DK)
    diag = jnp.stack([s5d[:, h, :, h, :] for h in range(GLA_HEADS)], axis=1)
    return jnp.swapaxes(diag, 2, 3)


def _layer(x, p, l, s5_h0r, s5_h0i, gla_s0, past_k, past_v, past_logf, tm, s5_tl, gla_tl, gla_chunk, fox_tq, fox_tk):
    nb, seq, _ = x.shape
    n = nb * seq
    x2d = x.reshape(n, D_MODEL)
    u, qk, gv, gg, misc, fq, fk, fv, logf, logft = _in_proj(
        x2d, p['norm1_g'][l], p['w_pack'][l], p['wft'][l], p['fb_row'][l], p['fb_col'][l], tm)

    u_t = jnp.swapaxes(u.reshape(nb, seq, S5_WIDTH), 0, 1)
    o_s5_t, hr, hi = _s5(u_t, s5_h0r, s5_h0i, p['abar_re'][l], p['abar_im'][l], p['bmat'][l], p['cmat'][l],
                         p['s5_d'][l], p['glu_w'][l], p['glu_b'][l], s5_tl)
    o_s5 = jnp.swapaxes(o_s5_t, 0, 1).reshape(n, S5_WIDTH)

    b3 = lambda a: a.reshape(nb, seq, a.shape[-1])
    o_gla, s_t = _gla(b3(qk), b3(gv), b3(gg), b3(misc), p['a2p'][l], p['ab'][l], p['ng'][l],
                      _gla_state_to_kernel(gla_s0), gla_tl, gla_chunk)

    lft = jnp.swapaxes(logft.reshape(FOX_HEADS, nb, seq), 0, 1)
    if seq < fox_tk:
        lft = jnp.pad(lft, ((0, 0), (0, 0), (0, fox_tk - seq)))
    if past_k is None:
        o_fox, k_rows = _fox(b3(fq), b3(fk), b3(fv), lft, p['qg'][l], p['kg'][l], fox_tq, fox_tk)
    else:
        plen = past_k.shape[1]
        o_fox, k_rows = _fox(b3(fq), b3(fk), b3(fv), lft, p['qg'][l], p['kg'][l], fox_tq, fox_tk,
                             past_k.reshape(nb, plen, FOX_WIDTH), past_v.reshape(nb, plen, FOX_WIDTH),
                             jnp.swapaxes(past_logf, 1, 2))

    y = _out_mlp(x2d, o_s5, o_gla.reshape(n, GLA_WIDTH), o_fox.reshape(n, FOX_WIDTH),
                 p['w_out'][l], p['norm2_g'][l], p['w_up'][l], p['w_down'][l], tm)
    states = (hr.reshape(nb, S5_GROUPS, S5_STATE), hi.reshape(nb, S5_GROUPS, S5_STATE),
              _gla_state_from_kernel(s_t),
              k_rows.reshape(nb, seq, FOX_HEADS, FOX_HD), fv.reshape(nb, seq, FOX_HEADS, FOX_HD),
              logf.reshape(nb, seq, FOX_HEADS))
    return y.reshape(nb, seq, D_MODEL), states


def kernel(x_prompt, x_sample, state_s5_re, state_s5_im, state_gla, cache_fox_k, cache_fox_v, cache_fox_logf, norm1_g, w_in, s5_a_re, s5_a_im, s5_log_dt, s5_b_re, s5_b_im, s5_c_re, s5_c_im, s5_d, s5_glu_w, s5_glu_b, gla_a2, gla_a_bias, gla_norm_g, fox_q_norm_g, fox_k_norm_g, fox_f_bias, w_out, norm2_g, w_up, w_down):
    depth = w_in.shape[0]
    p = _pack_params(norm1_g, w_in, s5_a_re, s5_a_im, s5_log_dt, s5_b_re, s5_b_im, s5_c_re, s5_c_im, s5_d,
                     s5_glu_w, s5_glu_b, gla_a2, gla_a_bias, gla_norm_g, fox_q_norm_g, fox_k_norm_g, fox_f_bias,
                     w_out, norm2_g, w_up, w_down)
    bp, lp, _ = x_prompt.shape
    bs, ls, _ = x_sample.shape
    np_tok, ns_tok = bp * lp, bs * ls
    tm_p = min(512, np_tok)
    tm_s = min(512, ns_tok)
    p_chunk = min(GLA_CHUNK, lp)
    s_chunk = min(GLA_CHUNK, ls)
    p_tq = min(256, lp)
    s_tq = min(256, ls)

    yp, ys = x_prompt, x_sample
    p_st = [[] for _ in range(6)]
    s_st = [[] for _ in range(6)]
    z_s5 = jnp.zeros((bp, S5_GN), F32)
    z_gla = jnp.zeros((bp, GLA_HEADS, GLA_DK, GLA_DV), F32)
    for l in range(depth):
        yp, stp = _layer(yp, p, l, z_s5, z_s5, z_gla, None, None, None,
                         tm_p, min(64, lp), min(256, lp), p_chunk, p_tq, max(p_tq, LANES))
        ys, sts = _layer(ys, p, l, state_s5_re[l].reshape(bs, S5_GN), state_s5_im[l].reshape(bs, S5_GN),
                         state_gla[l], cache_fox_k[l], cache_fox_v[l], cache_fox_logf[l],
                         tm_s, min(64, ls), min(256, ls), s_chunk, s_tq, max(s_tq, LANES))
        for j in range(6):
            p_st[j].append(stp[j])
            s_st[j].append(sts[j])
    p_out = [jnp.stack(a) for a in p_st]
    s_out = [jnp.stack(a) for a in s_st]
    return (yp, ys, *p_out, *s_out)
```

```python
import functools
import math

import jax
import jax.numpy as jnp
import numpy as np
from jax import lax
from jax.experimental import pallas as pl
from jax.experimental.pallas import tpu as pltpu

F32 = jnp.float32
BF16 = jnp.bfloat16

D_MODEL = 1024
EPS = 1e-6
LANES = 128

S5_WIDTH = 256
S5_GROUPS = 16
S5_GROUP = 16
S5_STATE = 64
S5_GN = S5_GROUPS * S5_STATE
S5_LANE_CHUNK = 256

GLA_WIDTH = 256
GLA_HEADS = 4
GLA_DK = 32
GLA_DV = 64
GLA_KW = 128
GLA_RANK = 16
GLA_TAU = 16.0
GLA_CHUNK = 64

FOX_WIDTH = 512
FOX_HD = 64
FOX_HEADS = 8
NEG_BIG = -1e30
LOG2E = math.log2(math.e)

D_FF = 4096
FF_CHUNK = 1024

COL_U = 0
COL_QK = 256
COL_GV = 512
COL_GG = 768
COL_MISC = 1024
COL_FQ = 1152
COL_FK = 1664
COL_FV = 2176
N_PACK = 2688
MISC_FF = 0
MISC_GA = 8

VMEM_LIMIT = 56 * 1024 * 1024


def _cparams(sem):
    return pltpu.CompilerParams(dimension_semantics=sem, vmem_limit_bytes=VMEM_LIMIT)


def _const_spec(shape):
    nd = len(shape)
    return pl.BlockSpec(shape, lambda *_: (0,) * nd)


def _log_sigmoid(x):
    return jnp.minimum(x, 0.0) - jnp.log1p(jnp.exp(-jnp.abs(x)))


def _sigmoid(x):
    return 1.0 / (1.0 + jnp.exp(-x))


def _split2(x):
    hi = x.astype(BF16)
    lo = (x - hi.astype(F32)).astype(BF16)
    return hi, lo


def _split3(x):
    hi = x.astype(BF16)
    r = x - hi.astype(F32)
    mid = r.astype(BF16)
    lo = (r - mid.astype(F32)).astype(BF16)
    return hi, mid, lo


def _dot(a, b):
    return jnp.dot(a, b, preferred_element_type=F32)


def _dot_nt(a, b):
    return lax.dot_general(a, b, (((1,), (1,)), ((), ())), preferred_element_type=F32)


def _dot_tn(a, b):
    return lax.dot_general(a, b, (((0,), (0,)), ((), ())), preferred_element_type=F32)


def _segment_rms_scale(x, seg_ref, width):
    hi, lo = _split2(x * x)
    seg = seg_ref[...]
    ms = (_dot(hi, seg) + _dot(lo, seg)) * (1.0 / width)
    return lax.rsqrt(ms + EPS)


def _in_proj_kernel(x_ref, g_ref, w_ref, fb_row_ref,
                    u_ref, qk_ref, gv_ref, gg_ref, misc_ref, fq_ref, fk_ref, fv_ref, logf_ref):
    x = x_ref[...]
    ms = jnp.mean(x * x, axis=-1, keepdims=True)
    h = (x * lax.rsqrt(ms + EPS) * g_ref[...]).astype(BF16)
    for ref, c0 in ((u_ref, COL_U), (qk_ref, COL_QK), (gv_ref, COL_GV), (gg_ref, COL_GG),
                    (fq_ref, COL_FQ), (fk_ref, COL_FK), (fv_ref, COL_FV)):
        ref[...] = _dot(h, w_ref[:, c0:c0 + ref.shape[1]])
    misc = _dot(h, w_ref[:, COL_MISC:COL_MISC + LANES])
    misc_ref[...] = misc
    logf_ref[...] = _log_sigmoid(misc[:, MISC_FF:MISC_FF + FOX_HEADS] + fb_row_ref[...])


def _in_proj(x2d, g, w_pack, fb_row, tm):
    n = x2d.shape[0]
    row = lambda w: pl.BlockSpec((tm, w), lambda i: (i, 0))
    widths = (S5_WIDTH, 256, GLA_WIDTH, GLA_WIDTH, LANES, FOX_WIDTH, FOX_WIDTH, FOX_WIDTH, FOX_HEADS)
    return pl.pallas_call(
        _in_proj_kernel,
        grid=(n // tm,),
        in_specs=[row(D_MODEL), _const_spec((1, D_MODEL)), _const_spec((D_MODEL, N_PACK)),
                  _const_spec((1, FOX_HEADS))],
        out_specs=[row(w) for w in widths],
        out_shape=[jax.ShapeDtypeStruct((n, w), F32) for w in widths],
        compiler_params=_cparams(("parallel",)),
        name="in_proj",
    )(x2d, g, w_pack, fb_row)


def _gelu_tanh(x):
    c = math.sqrt(2.0 / math.pi)
    return x * (0.5 * (1.0 + jnp.tanh(c * (x + 0.044715 * (x * x * x)))))


def _s5_kernel(u_ref, h0r_ref, h0i_ref, ar_ref, ai_ref, bmat_ref, cmat_ref, d_ref, gw_ref, gb_ref,
               o_ref, hr_out_ref, hi_out_ref, hbuf, hr_s, hi_s, *, tl, nb):
    i = pl.program_id(0)

    @pl.when(i == 0)
    def _():
        hr_s[...] = h0r_ref[...]
        hi_s[...] = h0i_ref[...]

    rows = tl * nb
    u = u_ref[...].reshape(rows, S5_WIDTH)
    hbuf[...] = _dot(u.astype(BF16), bmat_ref[...])

    for c in range(S5_GN // S5_LANE_CHUNK):
        lo = c * S5_LANE_CHUNK
        re = slice(lo, lo + S5_LANE_CHUNK)
        im = slice(S5_GN + lo, S5_GN + lo + S5_LANE_CHUNK)
        ar = jnp.broadcast_to(ar_ref[:, re], (nb, S5_LANE_CHUNK))
        ai = jnp.broadcast_to(ai_ref[:, re], (nb, S5_LANE_CHUNK))

        def body(t, carry, re=re, im=im, ar=ar, ai=ai):
            hr, hi = carry
            r = pl.ds(pl.multiple_of(t * nb, nb), nb)
            nhr = ar * hr - ai * hi + hbuf[r, re]
            nhi = ar * hi + ai * hr + hbuf[r, im]
            hbuf[r, re] = nhr
            hbuf[r, im] = nhi
            return nhr, nhi

        hr, hi = lax.fori_loop(0, tl, body, (hr_s[:, re], hi_s[:, re]), unroll=min(tl, 8))
        hr_s[:, re] = hr
        hi_s[:, re] = hi

    y = _dot(hbuf[...].astype(BF16), cmat_ref[...]) + d_ref[...] * u
    y = _gelu_tanh(y)
    z = _dot(y.astype(BF16), gw_ref[...]) + gb_ref[...]
    o_ref[...] = (y * _sigmoid(z)).reshape(tl, nb, S5_WIDTH)
    hr_out_ref[...] = hr_s[...]
    hi_out_ref[...] = hi_s[...]


def _s5(u_t, h0r, h0i, ar, ai, bmat, cmat, d, gw, gb, tl):
    seq, nb, _ = u_t.shape
    kern = functools.partial(_s5_kernel, tl=tl, nb=nb)
    state = _const_spec((nb, S5_GN))
    return pl.pallas_call(
        kern,
        grid=(seq // tl,),
        in_specs=[pl.BlockSpec((tl, nb, S5_WIDTH), lambda i: (i, 0, 0)), state, state,
                  _const_spec((1, S5_GN)), _const_spec((1, S5_GN)),
                  _const_spec((S5_WIDTH, 2 * S5_GN)), _const_spec((2 * S5_GN, S5_WIDTH)),
                  _const_spec((1, S5_WIDTH)), _const_spec((S5_WIDTH, S5_WIDTH)), _const_spec((1, S5_WIDTH))],
        out_specs=[pl.BlockSpec((tl, nb, S5_WIDTH), lambda i: (i, 0, 0)), state, state],
        out_shape=[jax.ShapeDtypeStruct((seq, nb, S5_WIDTH), F32),
                   jax.ShapeDtypeStruct((nb, S5_GN), F32), jax.ShapeDtypeStruct((nb, S5_GN), F32)],
        scratch_shapes=[pltpu.VMEM((tl * nb, 2 * S5_GN), F32),
                        pltpu.VMEM((nb, S5_GN), F32), pltpu.VMEM((nb, S5_GN), F32)],
        compiler_params=_cparams(("arbitrary",)),
        name="s5",
    )(u_t, h0r, h0i, ar, ai, bmat, cmat, d, gw, gb)


def _gla_kernel(qk_ref, v_ref, g_ref, misc_ref, a2_ref, ab_ref, ng_ref, s0_ref,
                tril_ref, mk_ref, mv_ref, ms_ref, causal_ref, seg_ref,
                o_ref, sfin_ref, s_scr, *, tl, chunk):
    i = pl.program_id(1)

    @pl.when(i == 0)
    def _():
        s_scr[...] = s0_ref[0]

    qk = qk_ref[0]
    q = qk[:, :GLA_KW] * (GLA_DK ** -0.5)
    k = qk[:, GLA_KW:]
    v = v_ref[0]
    la = _log_sigmoid(_dot(misc_ref[0].astype(BF16), a2_ref[...]) + ab_ref[...]) * (1.0 / GLA_TAU)

    tril = tril_ref[...]
    b = sum(_dot(tril, part) for part in _split3(la))
    nchunk = tl // chunk
    b3 = b.reshape(nchunk, chunk, GLA_KW)
    b_last = b3[:, chunk - 1:chunk, :]
    b_end = (b_last - b3).reshape(tl, GLA_KW)
    q_dec = (q * jnp.exp(b)).astype(BF16)
    k_inv = k * jnp.exp(-b)
    k_end = (k * jnp.exp(b_end)).astype(BF16)
    e_last = jnp.exp(b_last)

    outs = []
    for c in range(nchunk):
        sl = slice(c * chunk, (c + 1) * chunk)
        qd = q_dec[sl]
        kbd = jnp.where(mk_ref[...] > 0, jnp.concatenate([k_inv[sl]] * GLA_HEADS, axis=0), 0.0).astype(BF16)
        att = jnp.where(causal_ref[...] > 0, _dot_nt(qd, kbd), 0.0)
        vc = v[sl]
        vbd = jnp.where(mv_ref[...] > 0, jnp.concatenate([vc] * GLA_HEADS, axis=0), 0.0).astype(BF16)
        s = s_scr[...]
        outs.append(_dot(att.astype(BF16), vbd) + _dot_nt(qd, s.astype(BF16)))
        upd = jnp.where(ms_ref[...] > 0, _dot_tn(vc.astype(BF16), k_end[sl]), 0.0)
        s_scr[...] = s * e_last[c] + upd
    o = outs[0] if nchunk == 1 else jnp.concatenate(outs, axis=0)

    o = o * _segment_rms_scale(o, seg_ref, GLA_DV) * ng_ref[...]
    g = g_ref[0]
    o_ref[0] = o * (g * _sigmoid(g))
    sfin_ref[0] = s_scr[...]


def _gla_masks(tl, chunk):
    t = np.arange(tl)
    tril = ((t[:, None] // chunk == t[None, :] // chunk) & (t[None, :] <= t[:, None])).astype(np.float32)
    r = np.arange(GLA_HEADS * chunk)
    mk = (r[:, None] // chunk == np.arange(GLA_KW)[None, :] // GLA_DK).astype(np.float32)
    mv = (r[:, None] // chunk == np.arange(GLA_WIDTH)[None, :] // GLA_DV).astype(np.float32)
    ms = (np.arange(GLA_WIDTH)[:, None] // GLA_DV == np.arange(GLA_KW)[None, :] // GLA_DK).astype(np.float32)
    causal = (np.arange(chunk)[:, None] >= (r[None, :] % chunk)).astype(np.float32)
    seg = (np.arange(GLA_WIDTH)[:, None] // GLA_DV == np.arange(GLA_WIDTH)[None, :] // GLA_DV).astype(np.float32)
    return (jnp.asarray(tril, BF16), jnp.asarray(mk), jnp.asarray(mv), jnp.asarray(ms), jnp.asarray(causal),
            jnp.asarray(seg, BF16))


def _gla(qk, v, g, misc, a2p, ab, ng, s0t, tl, chunk):
    nb, seq, _ = qk.shape
    kern = functools.partial(_gla_kernel, tl=tl, chunk=chunk)
    masks = _gla_masks(tl, chunk)
    tok = lambda w: pl.BlockSpec((1, tl, w), lambda b, i: (b, i, 0))
    st = pl.BlockSpec((1, GLA_WIDTH, GLA_KW), lambda b, i: (b, 0, 0))
    return pl.pallas_call(
        kern,
        grid=(nb, seq // tl),
        in_specs=[tok(256), tok(GLA_WIDTH), tok(GLA_WIDTH), tok(LANES),
                  _const_spec((LANES, GLA_KW)), _const_spec((1, GLA_KW)), _const_spec((1, GLA_WIDTH)), st]
                 + [_const_spec(m.shape) for m in masks],
        out_specs=[tok(GLA_WIDTH), st],
        out_shape=[jax.ShapeDtypeStruct((nb, seq, GLA_WIDTH), F32),
                   jax.ShapeDtypeStruct((nb, GLA_WIDTH, GLA_KW), F32)],
        scratch_shapes=[pltpu.VMEM((GLA_WIDTH, GLA_KW), F32)],
        compiler_params=_cparams(("parallel", "arbitrary")),
        name="gla",
    )(qk, v, g, misc, a2p, ab, ng, s0t, *masks)


FOX_KAUG = 2 * FOX_HD
FOX_XCOLS = FOX_WIDTH + 3 * LANES
FOX_HEADS_PER_STEP = 4


def _fox_constants():
    sel_k = np.zeros((FOX_XCOLS, FOX_HEADS * FOX_KAUG), np.float32)
    sel_q = np.zeros((FOX_HEADS * FOX_KAUG, FOX_WIDTH), np.float32)
    ones_q = np.zeros((FOX_HEADS * FOX_KAUG, 1), np.float32)
    for h in range(FOX_HEADS):
        for d in range(FOX_HD):
            sel_k[h * FOX_HD + d, h * FOX_KAUG + d] = 1.0
            sel_q[h * FOX_KAUG + d, h * FOX_HD + d] = 1.0
        for part in range(3):
            sel_k[FOX_WIDTH + part * LANES + h, h * FOX_KAUG + FOX_HD + part] = 1.0
            ones_q[h * FOX_KAUG + FOX_HD + part, 0] = 1.0
    eye = np.eye(FOX_WIDTH, dtype=np.float32)
    t = np.arange(FOX_WIDTH)
    seg = (t[:, None] // FOX_HD == t[None, :] // FOX_HD).astype(np.float32)
    return (jnp.asarray(sel_k, BF16), jnp.asarray(sel_q, BF16), jnp.asarray(ones_q), jnp.asarray(eye, BF16),
            jnp.asarray(seg, BF16))


def _lower_ones(n):
    t = np.arange(n)
    return jnp.asarray((t[None, :] <= t[:, None]).astype(np.float32), BF16)


def _fox_kernel(*refs, tq, tqp, tk, seq, past, tkp):
    if past:
        (fq_ref, fk_ref, fv_ref, lf_ref, qg_ref, kg_ref, seg_ref, selk_ref, selq_ref, onesq_ref, eye_ref, tril_ref,
         pk_ref, pv_ref, plf_ref, trilp_ref, o_ref, krow_ref, kaug, vt, pkaug, pvt) = refs
    else:
        (fq_ref, fk_ref, fv_ref, lf_ref, qg_ref, kg_ref, seg_ref, selk_ref, selq_ref, onesq_ref, eye_ref, tril_ref,
         o_ref, krow_ref, kaug, vt) = refs
    i = pl.program_id(1)
    blk = min(tk, seq)

    def stage_block(k_bf, v_bf, logf, carry, tril, kaug_dst, vt_dst):
        n = logf.shape[0]
        logf128 = jnp.concatenate([logf, jnp.zeros((n, LANES - FOX_HEADS), F32)], axis=1)
        c = sum(_dot(tril, part) for part in _split3(logf128)) + carry
        x = jnp.concatenate([k_bf, *_split3(c * (-LOG2E))], axis=1)
        kaug_dst[0:n, :] = _dot(x, selk_ref[...]).astype(BF16)
        vt_dst[:, 0:n] = _dot_nt(eye_ref[...], v_bf).astype(BF16)
        return c[n - 1:n, :]

    @pl.when(i == 0)
    def _prologue():
        carry = jnp.zeros((1, LANES), F32)
        if past:
            for jb in range(past // tkp):
                r = slice(jb * tkp, (jb + 1) * tkp)
                carry = stage_block(pk_ref[0, r, :].astype(BF16), pv_ref[0, r, :].astype(BF16), plf_ref[0, r, :],
                                    carry, trilp_ref[...], pkaug.at[jb], pvt.at[jb])
        if blk < tk:
            kaug[...] = jnp.zeros_like(kaug)
            vt[...] = jnp.zeros_like(vt)
        for jb in range(max(seq // tk, 1)):
            r = slice(jb * tk, jb * tk + blk)
            kf = fk_ref[0, r, :]
            kn = kf * _segment_rms_scale(kf, seg_ref, FOX_HD) * kg_ref[...]
            krow_ref[0, r, :] = kn
            carry = stage_block(kn.astype(BF16), fv_ref[0, r, :].astype(BF16), lf_ref[0, r, :],
                                carry, tril_ref[...], kaug.at[jb], vt.at[jb])

    qf = fq_ref[0]
    qn = (qf * _segment_rms_scale(qf, seg_ref, FOX_HD) * qg_ref[...] * (FOX_HD ** -0.5 * LOG2E)).astype(BF16)
    if tqp > tq:
        qn = jnp.concatenate([qn, jnp.zeros((tqp - tq, FOX_WIDTH), BF16)], axis=0)
    qt_all = (_dot_nt(selq_ref[...], qn) + onesq_ref[...]).astype(BF16)
    key_i = lax.broadcasted_iota(jnp.int32, (tk, tqp), 0)
    qry_i = lax.broadcasted_iota(jnp.int32, (tk, tqp), 1)
    diag_ok = key_i <= qry_i

    def softmax_pv(state, s, v_t, mask):
        m, l, acc = state
        if mask is not None:
            s = jnp.where(mask, s, NEG_BIG)
        m_new = jnp.maximum(m, jnp.max(s, axis=0, keepdims=True))
        p = jnp.exp2(s - m_new)
        alpha = jnp.exp2(m - m_new)
        l = alpha * l + jnp.sum(p, axis=0, keepdims=True)
        acc = alpha * acc + _dot(v_t, p.astype(BF16))
        return m_new, l, acc

    klanes = [slice(h * FOX_KAUG, (h + 1) * FOX_KAUG) for h in range(FOX_HEADS)]
    vrows = [slice(h * FOX_HD, (h + 1) * FOX_HD) for h in range(FOX_HEADS)]
    outs = []
    for g in range(FOX_HEADS // FOX_HEADS_PER_STEP):
        heads = tuple(range(g * FOX_HEADS_PER_STEP, (g + 1) * FOX_HEADS_PER_STEP))
        qts = [qt_all[klanes[h], :] for h in heads]
        states = []
        for h, qt in zip(heads, qts):
            state = (jnp.full((1, tqp), NEG_BIG, F32), jnp.zeros((1, tqp), F32), jnp.zeros((FOX_HD, tqp), F32))
            if past:
                for jb in range(past // tkp):
                    state = softmax_pv(state, _dot(pkaug[jb, :, klanes[h]], qt), pvt[jb, vrows[h], :], None)
            states.append((*state, _dot(kaug[0, :, klanes[h]], qt)))

        def body(j, carries, heads=heads, qts=qts):
            out = []
            for h, qt, (m, l, acc, s) in zip(heads, qts, carries):
                s_next = _dot(kaug[j + 1, :, klanes[h]], qt)
                out.append((*softmax_pv((m, l, acc), s, vt[j, vrows[h], :], None), s_next))
            return tuple(out)

        carries = lax.fori_loop(0, i, body, tuple(states))
        for h, (m, l, acc, s) in zip(heads, carries):
            _, l, acc = softmax_pv((m, l, acc), s, vt[i, vrows[h], :], diag_ok)
            outs.append(acc / l)
    o_t = jnp.concatenate(outs, axis=0)
    o_ref[0] = o_t.T[0:tq, :]


def _fox(fq, fk, fv, logf, qg, kg, tq, tk, past_k=None, past_v=None, past_logf=None):
    nb, seq, _ = fq.shape
    nq = seq // tq
    tqp = max(tq, LANES)
    assert tqp == tk and (tq == tk or nq == 1)
    nkb = max(seq // tk, 1)
    past = 0 if past_k is None else past_k.shape[1]
    tkp = min(past, 256) if past else 0
    sel_k, sel_q, ones_q, eye, seg = _fox_constants()
    kern = functools.partial(_fox_kernel, tq=tq, tqp=tqp, tk=tk, seq=seq, past=past, tkp=tkp)
    full = lambda rows, w: pl.BlockSpec((1, rows, w), lambda b, i: (b, 0, 0))
    consts = [qg, kg, seg, sel_k, sel_q, ones_q, eye, _lower_ones(min(tk, seq))]
    in_specs = [pl.BlockSpec((1, tq, FOX_WIDTH), lambda b, i: (b, i, 0)), full(seq, FOX_WIDTH), full(seq, FOX_WIDTH),
                full(seq, FOX_HEADS)] + [_const_spec(c.shape) for c in consts]
    args = [fq, fk, fv, logf] + consts
    scratch = [pltpu.VMEM((nkb, tk, FOX_HEADS * FOX_KAUG), BF16), pltpu.VMEM((nkb, FOX_WIDTH, tk), BF16)]
    if past:
        in_specs += [full(past, FOX_WIDTH), full(past, FOX_WIDTH), full(past, FOX_HEADS), _const_spec((tkp, tkp))]
        args += [past_k, past_v, past_logf, _lower_ones(tkp)]
        scratch += [pltpu.VMEM((past // tkp, tkp, FOX_HEADS * FOX_KAUG), BF16),
                    pltpu.VMEM((past // tkp, FOX_WIDTH, tkp), BF16)]
    return pl.pallas_call(
        kern,
        grid=(nb, nq),
        in_specs=in_specs,
        out_specs=[pl.BlockSpec((1, tq, FOX_WIDTH), lambda b, i: (b, i, 0)), full(seq, FOX_WIDTH)],
        out_shape=[jax.ShapeDtypeStruct((nb, seq, FOX_WIDTH), F32), jax.ShapeDtypeStruct((nb, seq, FOX_WIDTH), F32)],
        scratch_shapes=scratch,
        compiler_params=_cparams(("parallel", "arbitrary")),
        name="fox",
    )(*args)


def _out_mlp_kernel(x_ref, s5_ref, gla_ref, fox_ref, wo_ref, g2_ref, wu_ref, wd_ref, y_ref, up_s):
    x1 = (x_ref[...]
          + _dot(s5_ref[...].astype(BF16), wo_ref[0:S5_WIDTH, :])
          + _dot(gla_ref[...].astype(BF16), wo_ref[S5_WIDTH:S5_WIDTH + GLA_WIDTH, :])
          + _dot(fox_ref[...].astype(BF16), wo_ref[S5_WIDTH + GLA_WIDTH:, :]))
    ms = jnp.mean(x1 * x1, axis=-1, keepdims=True)
    h2 = (x1 * lax.rsqrt(ms + EPS) * g2_ref[...]).astype(BF16)
    for c in range(D_FF // FF_CHUNK):
        cols = slice(c * FF_CHUNK, (c + 1) * FF_CHUNK)
        up = jnp.maximum(_dot(h2, wu_ref[:, cols]), 0.0)
        up_s[:, cols] = (up * up).astype(BF16)
    y_ref[...] = x1 + _dot(up_s[...], wd_ref[...])


def _out_mlp(x2d, s5, gla, fox, wo, g2, wu, wd, tm):
    n = x2d.shape[0]
    row = lambda w: pl.BlockSpec((tm, w), lambda i: (i, 0))
    resident = pl.BlockSpec(memory_space=pltpu.VMEM)
    return pl.pallas_call(
        _out_mlp_kernel,
        grid=(n // tm,),
        in_specs=[row(D_MODEL), row(S5_WIDTH), row(GLA_WIDTH), row(FOX_WIDTH),
                  resident, _const_spec((1, D_MODEL)), resident, resident],
        out_specs=row(D_MODEL),
        out_shape=jax.ShapeDtypeStruct((n, D_MODEL), F32),
        scratch_shapes=[pltpu.VMEM((tm, D_FF), BF16)],
        compiler_params=_cparams(("parallel",)),
        name="out_mlp",
    )(x2d, s5, gla, fox, wo, g2, wu, wd)


def _pack_params(norm1_g, w_in, s5_a_re, s5_a_im, s5_log_dt, s5_b_re, s5_b_im, s5_c_re, s5_c_im, s5_d,
                 s5_glu_w, s5_glu_b, gla_a2, gla_a_bias, gla_norm_g, fox_q_norm_g, fox_k_norm_g, fox_f_bias,
                 w_out, norm2_g, w_up, w_down):
    depth = w_in.shape[0]
    o_ga = 1024
    o_fq = o_ga + GLA_RANK
    o_ff = o_fq + 3 * FOX_WIDTH
    w_ff = w_in[:, :, o_ff:o_ff + FOX_HEADS]
    pad = jnp.zeros((depth, D_MODEL, LANES - FOX_HEADS - GLA_RANK), F32)
    w_pack = jnp.concatenate([w_in[:, :, :o_ga], w_ff, w_in[:, :, o_ga:o_fq], pad, w_in[:, :, o_fq:o_ff]],
                             axis=-1).astype(BF16)
    a2p = jnp.zeros((depth, LANES, GLA_KW), F32).at[:, MISC_GA:MISC_GA + GLA_RANK, :].set(gla_a2).astype(BF16)

    dt = jnp.exp(s5_log_dt)[..., None]
    mag = jnp.exp(dt * s5_a_re)
    abar_re = mag * jnp.cos(dt * s5_a_im)
    abar_im = mag * jnp.sin(dt * s5_a_im)
    den = s5_a_re * s5_a_re + s5_a_im * s5_a_im
    zr = ((abar_re - 1.0) * s5_a_re + abar_im * s5_a_im) / den
    zi = (abar_im * s5_a_re - (abar_re - 1.0) * s5_a_im) / den
    bb_re = zr[..., None] * s5_b_re - zi[..., None] * s5_b_im
    bb_im = zr[..., None] * s5_b_im + zi[..., None] * s5_b_re
    eye = jnp.eye(S5_GROUPS, dtype=F32)
    to_b = lambda bb: jnp.einsum('lgnp,gh->lgphn', bb, eye).reshape(depth, S5_WIDTH, S5_GN)
    bmat = jnp.concatenate([to_b(bb_re), to_b(bb_im)], axis=-1).astype(BF16)
    to_c = lambda cc: jnp.einsum('lgpn,gh->lgnhp', cc, eye).reshape(depth, S5_GN, S5_WIDTH)
    cmat = jnp.concatenate([to_c(s5_c_re), -to_c(s5_c_im)], axis=1).astype(BF16)

    return dict(
        norm1_g=norm1_g[:, None, :], w_pack=w_pack, fb_row=fox_f_bias[:, None, :],
        abar_re=abar_re.reshape(depth, 1, S5_GN), abar_im=abar_im.reshape(depth, 1, S5_GN),
        bmat=bmat, cmat=cmat, s5_d=s5_d.reshape(depth, 1, S5_WIDTH),
        glu_w=s5_glu_w.astype(BF16), glu_b=s5_glu_b[:, None, :],
        a2p=a2p, ab=gla_a_bias[:, None, :], ng=jnp.tile(gla_norm_g, (1, GLA_HEADS))[:, None, :],
        qg=jnp.tile(fox_q_norm_g, (1, FOX_HEADS))[:, None, :], kg=jnp.tile(fox_k_norm_g, (1, FOX_HEADS))[:, None, :],
        w_out=w_out.astype(BF16), norm2_g=norm2_g[:, None, :], w_up=w_up.astype(BF16), w_down=w_down.astype(BF16),
    )


def _gla_state_to_kernel(s):
    eye = jnp.eye(GLA_HEADS, dtype=s.dtype)
    return jnp.einsum('bhkv,hg->bhvgk', s, eye).reshape(s.shape[0], GLA_WIDTH, GLA_KW)


def _gla_state_from_kernel(st):
    nb = st.shape[0]
    s5d = st.reshape(nb, GLA_HEADS, GLA_DV, GLA_HEADS, GLA_DK)
    diag = jnp.stack([s5d[:, h, :, h, :] for h in range(GLA_HEADS)], axis=1)
    return jnp.swapaxes(diag, 2, 3)


def _layer(x, p, l, s5_h0r, s5_h0i, gla_s0, past_k, past_v, past_logf, tm, s5_tl, gla_tl, gla_chunk, fox_tq, fox_tk):
    nb, seq, _ = x.shape
    n = nb * seq
    x2d = x.reshape(n, D_MODEL)
    u, qk, gv, gg, misc, fq, fk, fv, logf = _in_proj(x2d, p['norm1_g'][l], p['w_pack'][l], p['fb_row'][l], tm)

    u_t = jnp.swapaxes(u.reshape(nb, seq, S5_WIDTH), 0, 1)
    o_s5_t, hr, hi = _s5(u_t, s5_h0r, s5_h0i, p['abar_re'][l], p['abar_im'][l], p['bmat'][l], p['cmat'][l],
                         p['s5_d'][l], p['glu_w'][l], p['glu_b'][l], s5_tl)
    o_s5 = jnp.swapaxes(o_s5_t, 0, 1).reshape(n, S5_WIDTH)

    b3 = lambda a: a.reshape(nb, seq, a.shape[-1])
    o_gla, s_t = _gla(b3(qk), b3(gv), b3(gg), b3(misc), p['a2p'][l], p['ab'][l], p['ng'][l],
                      _gla_state_to_kernel(gla_s0), gla_tl, gla_chunk)

    if past_k is None:
        o_fox, k_rows = _fox(b3(fq), b3(fk), b3(fv), b3(logf), p['qg'][l], p['kg'][l], fox_tq, fox_tk)
    else:
        plen = past_k.shape[1]
        o_fox, k_rows = _fox(b3(fq), b3(fk), b3(fv), b3(logf), p['qg'][l], p['kg'][l], fox_tq, fox_tk,
                             past_k.reshape(nb, plen, FOX_WIDTH), past_v.reshape(nb, plen, FOX_WIDTH), past_logf)

    y = _out_mlp(x2d, o_s5, o_gla.reshape(n, GLA_WIDTH), o_fox.reshape(n, FOX_WIDTH),
                 p['w_out'][l], p['norm2_g'][l], p['w_up'][l], p['w_down'][l], tm)
    states = (hr.reshape(nb, S5_GROUPS, S5_STATE), hi.reshape(nb, S5_GROUPS, S5_STATE),
              _gla_state_from_kernel(s_t),
              k_rows.reshape(nb, seq, FOX_HEADS, FOX_HD), fv.reshape(nb, seq, FOX_HEADS, FOX_HD),
              logf.reshape(nb, seq, FOX_HEADS))
    return y.reshape(nb, seq, D_MODEL), states


def kernel(x_prompt, x_sample, state_s5_re, state_s5_im, state_gla, cache_fox_k, cache_fox_v, cache_fox_logf, norm1_g, w_in, s5_a_re, s5_a_im, s5_log_dt, s5_b_re, s5_b_im, s5_c_re, s5_c_im, s5_d, s5_glu_w, s5_glu_b, gla_a2, gla_a_bias, gla_norm_g, fox_q_norm_g, fox_k_norm_g, fox_f_bias, w_out, norm2_g, w_up, w_down):
    depth = w_in.shape[0]
    p = _pack_params(norm1_g, w_in, s5_a_re, s5_a_im, s5_log_dt, s5_b_re, s5_b_im, s5_c_re, s5_c_im, s5_d,
                     s5_glu_w, s5_glu_b, gla_a2, gla_a_bias, gla_norm_g, fox_q_norm_g, fox_k_norm_g, fox_f_bias,
                     w_out, norm2_g, w_up, w_down)
    bp, lp, _ = x_prompt.shape
    bs, ls, _ = x_sample.shape
    np_tok, ns_tok = bp * lp, bs * ls
    tm_p = min(512, np_tok)
    tm_s = min(512, ns_tok)
    p_chunk = min(GLA_CHUNK, lp)
    s_chunk = min(GLA_CHUNK, ls)
    p_tq = min(256, lp)
    s_tq = min(256, ls)

    yp, ys = x_prompt, x_sample
    p_st = [[] for _ in range(6)]
    s_st = [[] for _ in range(6)]
    z_s5 = jnp.zeros((bp, S5_GN), F32)
    z_gla = jnp.zeros((bp, GLA_HEADS, GLA_DK, GLA_DV), F32)
    for l in range(depth):
        yp, stp = _layer(yp, p, l, z_s5, z_s5, z_gla, None, None, None,
                         tm_p, min(64, lp), min(256, lp), p_chunk, p_tq, max(p_tq, LANES))
        ys, sts = _layer(ys, p, l, state_s5_re[l].reshape(bs, S5_GN), state_s5_im[l].reshape(bs, S5_GN),
                         state_gla[l], cache_fox_k[l], cache_fox_v[l], cache_fox_logf[l],
                         tm_s, min(64, ls), min(256, ls), s_chunk, s_tq, max(s_tq, LANES))
        for j in range(6):
            p_st[j].append(stp[j])
            s_st[j].append(sts[j])
    p_out = [jnp.stack(a) for a in p_st]
    s_out = [jnp.stack(a) for a in s_st]
    return (yp, ys, *p_out, *s_out)
```

```python
import functools
import math

import jax
import jax.numpy as jnp
import numpy as np
from jax import lax
from jax.experimental import pallas as pl
from jax.experimental.pallas import tpu as pltpu

F32 = jnp.float32
BF16 = jnp.bfloat16

D_MODEL = 1024
EPS = 1e-6
LANES = 128

S5_WIDTH = 256
S5_GROUPS = 16
S5_GROUP = 16
S5_STATE = 64
S5_GN = S5_GROUPS * S5_STATE
S5_LANE_CHUNK = 256

GLA_WIDTH = 256
GLA_HEADS = 4
GLA_DK = 32
GLA_DV = 64
GLA_KW = 128
GLA_RANK = 16
GLA_TAU = 16.0
GLA_CHUNK = 64
GLA_CUMSUM_SPAN = 256

FOX_WIDTH = 512
FOX_HD = 64
FOX_HEADS = 8
NEG_BIG = -1e30
LOG2E = math.log2(math.e)

D_FF = 4096
FF_CHUNK = 1024

COL_U = 0
COL_QK = 256
COL_GV = 512
COL_GG = 768
COL_MISC = 1024
COL_FQ = 1152
COL_FK = 1664
COL_FV = 2176
N_PACK = 2688
MISC_FF = 0
MISC_GA = 8

VMEM_LIMIT = 56 * 1024 * 1024


def _cparams(sem):
    return pltpu.CompilerParams(dimension_semantics=sem, vmem_limit_bytes=VMEM_LIMIT)


def _const_spec(shape):
    nd = len(shape)
    return pl.BlockSpec(shape, lambda *_: (0,) * nd)


def _log_sigmoid(x):
    return jnp.minimum(x, 0.0) - jnp.log1p(jnp.exp(-jnp.abs(x)))


def _sigmoid(x):
    return 1.0 / (1.0 + jnp.exp(-x))


def _split2(x):
    hi = x.astype(BF16)
    lo = (x - hi.astype(F32)).astype(BF16)
    return hi, lo


def _split3(x):
    hi = x.astype(BF16)
    r = x - hi.astype(F32)
    mid = r.astype(BF16)
    lo = (r - mid.astype(F32)).astype(BF16)
    return hi, mid, lo


def _dot(a, b):
    return jnp.dot(a, b, preferred_element_type=F32)


def _dot_nt(a, b):
    return lax.dot_general(a, b, (((1,), (1,)), ((), ())), preferred_element_type=F32)


def _dot_tn(a, b):
    return lax.dot_general(a, b, (((0,), (0,)), ((), ())), preferred_element_type=F32)


def _segment_rms_scale(x, seg_ref, width):
    hi, lo = _split2(x * x)
    seg = seg_ref[...]
    ms = (_dot(hi, seg) + _dot(lo, seg)) * (1.0 / width)
    return lax.rsqrt(ms + EPS)


def _in_proj_kernel(x_ref, g_ref, w_ref, fb_row_ref,
                    u_ref, qk_ref, gv_ref, gg_ref, misc_ref, fq_ref, fk_ref, fv_ref, logf_ref):
    x = x_ref[...]
    ms = jnp.mean(x * x, axis=-1, keepdims=True)
    h = (x * lax.rsqrt(ms + EPS) * g_ref[...]).astype(BF16)
    for ref, c0 in ((u_ref, COL_U), (qk_ref, COL_QK), (gv_ref, COL_GV), (gg_ref, COL_GG),
                    (fq_ref, COL_FQ), (fk_ref, COL_FK), (fv_ref, COL_FV)):
        ref[...] = _dot(h, w_ref[:, c0:c0 + ref.shape[1]])
    misc = _dot(h, w_ref[:, COL_MISC:COL_MISC + LANES])
    misc_ref[...] = misc
    logf_ref[...] = _log_sigmoid(misc[:, MISC_FF:MISC_FF + FOX_HEADS] + fb_row_ref[...])


def _in_proj(x2d, g, w_pack, fb_row, tm):
    n = x2d.shape[0]
    row = lambda w: pl.BlockSpec((tm, w), lambda i: (i, 0))
    widths = (S5_WIDTH, 256, GLA_WIDTH, GLA_WIDTH, LANES, FOX_WIDTH, FOX_WIDTH, FOX_WIDTH, FOX_HEADS)
    return pl.pallas_call(
        _in_proj_kernel,
        grid=(n // tm,),
        in_specs=[row(D_MODEL), _const_spec((1, D_MODEL)), _const_spec((D_MODEL, N_PACK)),
                  _const_spec((1, FOX_HEADS))],
        out_specs=[row(w) for w in widths],
        out_shape=[jax.ShapeDtypeStruct((n, w), F32) for w in widths],
        compiler_params=_cparams(("parallel",)),
        name="in_proj",
    )(x2d, g, w_pack, fb_row)


def _gelu_tanh(x):
    c = math.sqrt(2.0 / math.pi)
    return x * (0.5 * (1.0 + jnp.tanh(c * (x + 0.044715 * (x * x * x)))))


def _s5_kernel(u_ref, h0r_ref, h0i_ref, ar_ref, ai_ref, bmat_ref, cmat_ref, d_ref, gw_ref, gb_ref,
               o_ref, hr_out_ref, hi_out_ref, hbuf, hr_s, hi_s, *, tl, nb):
    i = pl.program_id(0)

    @pl.when(i == 0)
    def _():
        hr_s[...] = h0r_ref[...]
        hi_s[...] = h0i_ref[...]

    rows = tl * nb
    u = u_ref[...].reshape(rows, S5_WIDTH)
    hbuf[...] = _dot(u.astype(BF16), bmat_ref[...])

    for c in range(S5_GN // S5_LANE_CHUNK):
        lo = c * S5_LANE_CHUNK
        re = slice(lo, lo + S5_LANE_CHUNK)
        im = slice(S5_GN + lo, S5_GN + lo + S5_LANE_CHUNK)
        ar = jnp.broadcast_to(ar_ref[:, re], (nb, S5_LANE_CHUNK))
        ai = jnp.broadcast_to(ai_ref[:, re], (nb, S5_LANE_CHUNK))

        def body(t, carry, re=re, im=im, ar=ar, ai=ai):
            hr, hi = carry
            r = pl.ds(pl.multiple_of(t * nb, nb), nb)
            nhr = ar * hr - ai * hi + hbuf[r, re]
            nhi = ar * hi + ai * hr + hbuf[r, im]
            hbuf[r, re] = nhr
            hbuf[r, im] = nhi
            return nhr, nhi

        hr, hi = lax.fori_loop(0, tl, body, (hr_s[:, re], hi_s[:, re]), unroll=min(tl, 8))
        hr_s[:, re] = hr
        hi_s[:, re] = hi

    y = _dot(hbuf[...].astype(BF16), cmat_ref[...]) + d_ref[...] * u
    y = _gelu_tanh(y)
    z = _dot(y.astype(BF16), gw_ref[...]) + gb_ref[...]
    o_ref[...] = (y * _sigmoid(z)).reshape(tl, nb, S5_WIDTH)
    hr_out_ref[...] = hr_s[...]
    hi_out_ref[...] = hi_s[...]


def _s5(u_t, h0r, h0i, ar, ai, bmat, cmat, d, gw, gb, tl):
    seq, nb, _ = u_t.shape
    kern = functools.partial(_s5_kernel, tl=tl, nb=nb)
    state = _const_spec((nb, S5_GN))
    return pl.pallas_call(
        kern,
        grid=(seq // tl,),
        in_specs=[pl.BlockSpec((tl, nb, S5_WIDTH), lambda i: (i, 0, 0)), state, state,
                  _const_spec((1, S5_GN)), _const_spec((1, S5_GN)),
                  _const_spec((S5_WIDTH, 2 * S5_GN)), _const_spec((2 * S5_GN, S5_WIDTH)),
                  _const_spec((1, S5_WIDTH)), _const_spec((S5_WIDTH, S5_WIDTH)), _const_spec((1, S5_WIDTH))],
        out_specs=[pl.BlockSpec((tl, nb, S5_WIDTH), lambda i: (i, 0, 0)), state, state],
        out_shape=[jax.ShapeDtypeStruct((seq, nb, S5_WIDTH), F32),
                   jax.ShapeDtypeStruct((nb, S5_GN), F32), jax.ShapeDtypeStruct((nb, S5_GN), F32)],
        scratch_shapes=[pltpu.VMEM((tl * nb, 2 * S5_GN), F32),
                        pltpu.VMEM((nb, S5_GN), F32), pltpu.VMEM((nb, S5_GN), F32)],
        compiler_params=_cparams(("arbitrary",)),
        name="s5",
    )(u_t, h0r, h0i, ar, ai, bmat, cmat, d, gw, gb)


def _gla_kernel(qk_ref, v_ref, g_ref, misc_ref, a2_ref, ab_ref, ng_ref, s0_ref,
                tril_ref, mk_ref, mv_ref, ms_ref, causal_ref, seg_ref,
                o_ref, sfin_ref, s_scr, *, tl, chunk):
    i = pl.program_id(1)

    @pl.when(i == 0)
    def _():
        s_scr[...] = s0_ref[0]

    qk = qk_ref[0]
    q = qk[:, :GLA_KW] * (GLA_DK ** -0.5)
    k = qk[:, GLA_KW:]
    v = v_ref[0]
    la = _log_sigmoid(_dot(misc_ref[0].astype(BF16), a2_ref[...]) + ab_ref[...]) * (1.0 / GLA_TAU)

    tril = tril_ref[...]
    span = tril.shape[0]
    parts = _split3(la)
    b = jnp.concatenate([sum(_dot(tril, part[r0:r0 + span]) for part in parts) for r0 in range(0, tl, span)], axis=0)
    nchunk = tl // chunk
    b3 = b.reshape(nchunk, chunk, GLA_KW)
    b_last = b3[:, chunk - 1:chunk, :]
    b_end = (b_last - b3).reshape(tl, GLA_KW)
    q_dec = (q * jnp.exp(b)).astype(BF16)
    k_inv = k * jnp.exp(-b)
    k_end = (k * jnp.exp(b_end)).astype(BF16)
    e_last = jnp.exp(b_last)

    outs = []
    for c in range(nchunk):
        sl = slice(c * chunk, (c + 1) * chunk)
        qd = q_dec[sl]
        kbd = jnp.where(mk_ref[...] > 0, jnp.concatenate([k_inv[sl]] * GLA_HEADS, axis=0), 0.0).astype(BF16)
        att = jnp.where(causal_ref[...] > 0, _dot_nt(qd, kbd), 0.0)
        vc = v[sl]
        vbd = jnp.where(mv_ref[...] > 0, jnp.concatenate([vc] * GLA_HEADS, axis=0), 0.0).astype(BF16)
        s = s_scr[...]
        outs.append(_dot(att.astype(BF16), vbd) + _dot_nt(qd, s.astype(BF16)))
        upd = jnp.where(ms_ref[...] > 0, _dot_tn(vc.astype(BF16), k_end[sl]), 0.0)
        s_scr[...] = s * e_last[c] + upd
    o = outs[0] if nchunk == 1 else jnp.concatenate(outs, axis=0)

    o = o * _segment_rms_scale(o, seg_ref, GLA_DV) * ng_ref[...]
    g = g_ref[0]
    o_ref[0] = o * (g * _sigmoid(g))
    sfin_ref[0] = s_scr[...]


def _gla_masks(tl, chunk):
    t = np.arange(min(tl, GLA_CUMSUM_SPAN))
    tril = ((t[:, None] // chunk == t[None, :] // chunk) & (t[None, :] <= t[:, None])).astype(np.float32)
    r = np.arange(GLA_HEADS * chunk)
    mk = (r[:, None] // chunk == np.arange(GLA_KW)[None, :] // GLA_DK).astype(np.float32)
    mv = (r[:, None] // chunk == np.arange(GLA_WIDTH)[None, :] // GLA_DV).astype(np.float32)
    ms = (np.arange(GLA_WIDTH)[:, None] // GLA_DV == np.arange(GLA_KW)[None, :] // GLA_DK).astype(np.float32)
    causal = (np.arange(chunk)[:, None] >= (r[None, :] % chunk)).astype(np.float32)
    seg = (np.arange(GLA_WIDTH)[:, None] // GLA_DV == np.arange(GLA_WIDTH)[None, :] // GLA_DV).astype(np.float32)
    return (jnp.asarray(tril, BF16), jnp.asarray(mk), jnp.asarray(mv), jnp.asarray(ms), jnp.asarray(causal),
            jnp.asarray(seg, BF16))


def _gla(qk, v, g, misc, a2p, ab, ng, s0t, tl, chunk):
    nb, seq, _ = qk.shape
    kern = functools.partial(_gla_kernel, tl=tl, chunk=chunk)
    masks = _gla_masks(tl, chunk)
    tok = lambda w: pl.BlockSpec((1, tl, w), lambda b, i: (b, i, 0))
    st = pl.BlockSpec((1, GLA_WIDTH, GLA_KW), lambda b, i: (b, 0, 0))
    return pl.pallas_call(
        kern,
        grid=(nb, seq // tl),
        in_specs=[tok(256), tok(GLA_WIDTH), tok(GLA_WIDTH), tok(LANES),
                  _const_spec((LANES, GLA_KW)), _const_spec((1, GLA_KW)), _const_spec((1, GLA_WIDTH)), st]
                 + [_const_spec(m.shape) for m in masks],
        out_specs=[tok(GLA_WIDTH), st],
        out_shape=[jax.ShapeDtypeStruct((nb, seq, GLA_WIDTH), F32),
                   jax.ShapeDtypeStruct((nb, GLA_WIDTH, GLA_KW), F32)],
        scratch_shapes=[pltpu.VMEM((GLA_WIDTH, GLA_KW), F32)],
        compiler_params=_cparams(("parallel", "arbitrary")),
        name="gla",
    )(qk, v, g, misc, a2p, ab, ng, s0t, *masks)


FOX_KAUG = 2 * FOX_HD
FOX_HEADS_PER_STEP = 8


def _fox_constants():
    sel_k = np.zeros((FOX_HEADS // 2, 2 * LANES, 2 * FOX_KAUG), np.float32)
    sel_q = np.zeros((2 * FOX_KAUG, LANES), np.float32)
    ones_q = np.zeros((2 * FOX_KAUG, 1), np.float32)
    for hh in range(2):
        for d in range(FOX_HD):
            sel_k[:, hh * FOX_HD + d, hh * FOX_KAUG + d] = 1.0
            sel_q[hh * FOX_KAUG + d, hh * FOX_HD + d] = 1.0
        for part in range(3):
            ones_q[hh * FOX_KAUG + FOX_HD + part, 0] = 1.0
            for pair in range(FOX_HEADS // 2):
                sel_k[pair, LANES + part * FOX_HEADS + 2 * pair + hh, hh * FOX_KAUG + FOX_HD + part] = 1.0
    eye = np.eye(FOX_WIDTH, dtype=np.float32)
    t = np.arange(FOX_WIDTH)
    seg = (t[:, None] // FOX_HD == t[None, :] // FOX_HD).astype(np.float32)
    return (jnp.asarray(sel_k, BF16), jnp.asarray(sel_q, BF16), jnp.asarray(ones_q), jnp.asarray(eye, BF16),
            jnp.asarray(seg, BF16))


def _lower_ones(n):
    t = np.arange(n)
    return jnp.asarray((t[None, :] <= t[:, None]).astype(np.float32), BF16)


def _fox_kernel(*refs, tq, tqp, tk, seq, past, tkp):
    if past:
        (fq_ref, fk_ref, fv_ref, lf_ref, qg_ref, kg_ref, seg_ref, selk_ref, selq_ref, onesq_ref, eye_ref, tril_ref,
         pk_ref, pv_ref, plf_ref, trilp_ref, o_ref, krow_ref, kaug, vt, pkaug, pvt) = refs
    else:
        (fq_ref, fk_ref, fv_ref, lf_ref, qg_ref, kg_ref, seg_ref, selk_ref, selq_ref, onesq_ref, eye_ref, tril_ref,
         o_ref, krow_ref, kaug, vt) = refs
    i = pl.program_id(1)
    blk = min(tk, seq)

    def stage_block(k_bf, v_bf, logf, carry, tril, kaug_dst, vt_dst):
        n = logf.shape[0]
        logf128 = jnp.concatenate([logf, jnp.zeros((n, LANES - FOX_HEADS), F32)], axis=1)
        c = sum(_dot(tril, part) for part in _split3(logf128)) + carry
        hi, mid, lo = (part.astype(F32) for part in _split3(c * (-LOG2E)))
        packed = (hi + pltpu.roll(mid, FOX_HEADS, 1) + pltpu.roll(lo, 2 * FOX_HEADS, 1)).astype(BF16)
        for pair in range(FOX_HEADS // 2):
            x = jnp.concatenate([k_bf[:, pair * LANES:(pair + 1) * LANES], packed], axis=1)
            kaug_dst[0:n, pair * 2 * FOX_KAUG:(pair + 1) * 2 * FOX_KAUG] = _dot(x, selk_ref[pair]).astype(BF16)
        vt_dst[:, 0:n] = _dot_nt(eye_ref[...], v_bf).astype(BF16)
        return c[n - 1:n, :]

    @pl.when(i == 0)
    def _prologue():
        carry = jnp.zeros((1, LANES), F32)
        if past:
            for jb in range(past // tkp):
                r = slice(jb * tkp, (jb + 1) * tkp)
                carry = stage_block(pk_ref[0, r, :].astype(BF16), pv_ref[0, r, :].astype(BF16), plf_ref[0, r, :],
                                    carry, trilp_ref[...], pkaug.at[jb], pvt.at[jb])
        if blk < tk:
            kaug[...] = jnp.zeros_like(kaug)
            vt[...] = jnp.zeros_like(vt)
        for jb in range(max(seq // tk, 1)):
            r = slice(jb * tk, jb * tk + blk)
            kf = fk_ref[0, r, :]
            kn = kf * _segment_rms_scale(kf, seg_ref, FOX_HD) * kg_ref[...]
            krow_ref[0, r, :] = kn
            carry = stage_block(kn.astype(BF16), fv_ref[0, r, :].astype(BF16), lf_ref[0, r, :],
                                carry, tril_ref[...], kaug.at[jb], vt.at[jb])

    qf = fq_ref[0]
    qn = (qf * _segment_rms_scale(qf, seg_ref, FOX_HD) * qg_ref[...] * (FOX_HD ** -0.5 * LOG2E)).astype(BF16)
    if tqp > tq:
        qn = jnp.concatenate([qn, jnp.zeros((tqp - tq, FOX_WIDTH), BF16)], axis=0)
    qt_all = jnp.concatenate(
        [(_dot_nt(selq_ref[...], qn[:, pair * LANES:(pair + 1) * LANES]) + onesq_ref[...]).astype(BF16)
         for pair in range(FOX_HEADS // 2)], axis=0)
    key_i = lax.broadcasted_iota(jnp.int32, (tk, tqp), 0)
    qry_i = lax.broadcasted_iota(jnp.int32, (tk, tqp), 1)
    diag_ok = key_i <= qry_i

    def softmax_pv(state, s, v_t, mask):
        m, l, acc = state
        if mask is not None:
            s = jnp.where(mask, s, NEG_BIG)
        m_new = jnp.maximum(m, jnp.max(s, axis=0, keepdims=True))
        p = jnp.exp2(s - m_new)
        alpha = jnp.exp2(m - m_new)
        l = alpha * l + jnp.sum(p, axis=0, keepdims=True)
        acc = alpha * acc + _dot(v_t, p.astype(BF16))
        return m_new, l, acc

    klanes = [slice(h * FOX_KAUG, (h + 1) * FOX_KAUG) for h in range(FOX_HEADS)]
    vrows = [slice(h * FOX_HD, (h + 1) * FOX_HD) for h in range(FOX_HEADS)]
    outs = []
    for g in range(FOX_HEADS // FOX_HEADS_PER_STEP):
        heads = tuple(range(g * FOX_HEADS_PER_STEP, (g + 1) * FOX_HEADS_PER_STEP))
        qts = [qt_all[klanes[h], :] for h in heads]
        states = []
        for h, qt in zip(heads, qts):
            state = (jnp.full((1, tqp), NEG_BIG, F32), jnp.zeros((1, tqp), F32), jnp.zeros((FOX_HD, tqp), F32))
            if past:
                for jb in range(past // tkp):
                    state = softmax_pv(state, _dot(pkaug[jb, :, klanes[h]], qt), pvt[jb, vrows[h], :], None)
            states.append((*state, _dot(kaug[0, :, klanes[h]], qt)))

        def body(j, carries, heads=heads, qts=qts):
            out = []
            for h, qt, (m, l, acc, s) in zip(heads, qts, carries):
                s_next = _dot(kaug[j + 1, :, klanes[h]], qt)
                out.append((*softmax_pv((m, l, acc), s, vt[j, vrows[h], :], None), s_next))
            return tuple(out)

        carries = lax.fori_loop(0, i, body, tuple(states))
        for h, (m, l, acc, s) in zip(heads, carries):
            _, l, acc = softmax_pv((m, l, acc), s, vt[i, vrows[h], :], diag_ok)
            outs.append(acc / l)
    o_t = jnp.concatenate(outs, axis=0)
    o_ref[0] = o_t.T[0:tq, :]


def _fox(fq, fk, fv, logf, qg, kg, tq, tk, past_k=None, past_v=None, past_logf=None):
    nb, seq, _ = fq.shape
    nq = seq // tq
    tqp = max(tq, LANES)
    assert tqp == tk and (tq == tk or nq == 1)
    nkb = max(seq // tk, 1)
    past = 0 if past_k is None else past_k.shape[1]
    tkp = min(past, 256) if past else 0
    sel_k, sel_q, ones_q, eye, seg = _fox_constants()
    kern = functools.partial(_fox_kernel, tq=tq, tqp=tqp, tk=tk, seq=seq, past=past, tkp=tkp)
    full = lambda rows, w: pl.BlockSpec((1, rows, w), lambda b, i: (b, 0, 0))
    consts = [qg, kg, seg, sel_k, sel_q, ones_q, eye, _lower_ones(min(tk, seq))]
    in_specs = [pl.BlockSpec((1, tq, FOX_WIDTH), lambda b, i: (b, i, 0)), full(seq, FOX_WIDTH), full(seq, FOX_WIDTH),
                full(seq, FOX_HEADS)] + [_const_spec(c.shape) for c in consts]
    args = [fq, fk, fv, logf] + consts
    scratch = [pltpu.VMEM((nkb, tk, FOX_HEADS * FOX_KAUG), BF16), pltpu.VMEM((nkb, FOX_WIDTH, tk), BF16)]
    if past:
        in_specs += [full(past, FOX_WIDTH), full(past, FOX_WIDTH), full(past, FOX_HEADS), _const_spec((tkp, tkp))]
        args += [past_k, past_v, past_logf, _lower_ones(tkp)]
        scratch += [pltpu.VMEM((past // tkp, tkp, FOX_HEADS * FOX_KAUG), BF16),
                    pltpu.VMEM((past // tkp, FOX_WIDTH, tkp), BF16)]
    return pl.pallas_call(
        kern,
        grid=(nb, nq),
        in_specs=in_specs,
        out_specs=[pl.BlockSpec((1, tq, FOX_WIDTH), lambda b, i: (b, i, 0)), full(seq, FOX_WIDTH)],
        out_shape=[jax.ShapeDtypeStruct((nb, seq, FOX_WIDTH), F32), jax.ShapeDtypeStruct((nb, seq, FOX_WIDTH), F32)],
        scratch_shapes=scratch,
        compiler_params=_cparams(("parallel", "arbitrary")),
        name="fox",
    )(*args)


def _out_mlp_kernel(x_ref, s5_ref, gla_ref, fox_ref, wo_ref, g2_ref, wu_ref, wd_ref, y_ref, up_s):
    x1 = (x_ref[...]
          + _dot(s5_ref[...].astype(BF16), wo_ref[0:S5_WIDTH, :])
          + _dot(gla_ref[...].astype(BF16), wo_ref[S5_WIDTH:S5_WIDTH + GLA_WIDTH, :])
          + _dot(fox_ref[...].astype(BF16), wo_ref[S5_WIDTH + GLA_WIDTH:, :]))
    ms = jnp.mean(x1 * x1, axis=-1, keepdims=True)
    h2 = (x1 * lax.rsqrt(ms + EPS) * g2_ref[...]).astype(BF16)
    for c in range(D_FF // FF_CHUNK):
        cols = slice(c * FF_CHUNK, (c + 1) * FF_CHUNK)
        up = jnp.maximum(_dot(h2, wu_ref[:, cols]), 0.0)
        up_s[:, cols] = (up * up).astype(BF16)
    y_ref[...] = x1 + _dot(up_s[...], wd_ref[...])


def _out_mlp(x2d, s5, gla, fox, wo, g2, wu, wd, tm):
    n = x2d.shape[0]
    row = lambda w: pl.BlockSpec((tm, w), lambda i: (i, 0))
    resident = pl.BlockSpec(memory_space=pltpu.VMEM)
    return pl.pallas_call(
        _out_mlp_kernel,
        grid=(n // tm,),
        in_specs=[row(D_MODEL), row(S5_WIDTH), row(GLA_WIDTH), row(FOX_WIDTH),
                  resident, _const_spec((1, D_MODEL)), resident, resident],
        out_specs=row(D_MODEL),
        out_shape=jax.ShapeDtypeStruct((n, D_MODEL), F32),
        scratch_shapes=[pltpu.VMEM((tm, D_FF), BF16)],
        compiler_params=_cparams(("parallel",)),
        name="out_mlp",
    )(x2d, s5, gla, fox, wo, g2, wu, wd)


def _pack_params(norm1_g, w_in, s5_a_re, s5_a_im, s5_log_dt, s5_b_re, s5_b_im, s5_c_re, s5_c_im, s5_d,
                 s5_glu_w, s5_glu_b, gla_a2, gla_a_bias, gla_norm_g, fox_q_norm_g, fox_k_norm_g, fox_f_bias,
                 w_out, norm2_g, w_up, w_down):
    depth = w_in.shape[0]
    o_ga = 1024
    o_fq = o_ga + GLA_RANK
    o_ff = o_fq + 3 * FOX_WIDTH
    w_ff = w_in[:, :, o_ff:o_ff + FOX_HEADS]
    pad = jnp.zeros((depth, D_MODEL, LANES - FOX_HEADS - GLA_RANK), F32)
    w_pack = jnp.concatenate([w_in[:, :, :o_ga], w_ff, w_in[:, :, o_ga:o_fq], pad, w_in[:, :, o_fq:o_ff]],
                             axis=-1).astype(BF16)
    a2p = jnp.zeros((depth, LANES, GLA_KW), F32).at[:, MISC_GA:MISC_GA + GLA_RANK, :].set(gla_a2).astype(BF16)

    dt = jnp.exp(s5_log_dt)[..., None]
    mag = jnp.exp(dt * s5_a_re)
    abar_re = mag * jnp.cos(dt * s5_a_im)
    abar_im = mag * jnp.sin(dt * s5_a_im)
    den = s5_a_re * s5_a_re + s5_a_im * s5_a_im
    zr = ((abar_re - 1.0) * s5_a_re + abar_im * s5_a_im) / den
    zi = (abar_im * s5_a_re - (abar_re - 1.0) * s5_a_im) / den
    bb_re = zr[..., None] * s5_b_re - zi[..., None] * s5_b_im
    bb_im = zr[..., None] * s5_b_im + zi[..., None] * s5_b_re
    eye = jnp.eye(S5_GROUPS, dtype=F32)
    to_b = lambda bb: jnp.einsum('lgnp,gh->lgphn', bb, eye).reshape(depth, S5_WIDTH, S5_GN)
    bmat = jnp.concatenate([to_b(bb_re), to_b(bb_im)], axis=-1).astype(BF16)
    to_c = lambda cc: jnp.einsum('lgpn,gh->lgnhp', cc, eye).reshape(depth, S5_GN, S5_WIDTH)
    cmat = jnp.concatenate([to_c(s5_c_re), -to_c(s5_c_im)], axis=1).astype(BF16)

    return dict(
        norm1_g=norm1_g[:, None, :], w_pack=w_pack, fb_row=fox_f_bias[:, None, :],
        abar_re=abar_re.reshape(depth, 1, S5_GN), abar_im=abar_im.reshape(depth, 1, S5_GN),
        bmat=bmat, cmat=cmat, s5_d=s5_d.reshape(depth, 1, S5_WIDTH),
        glu_w=s5_glu_w.astype(BF16), glu_b=s5_glu_b[:, None, :],
        a2p=a2p, ab=gla_a_bias[:, None, :], ng=jnp.tile(gla_norm_g, (1, GLA_HEADS))[:, None, :],
        qg=jnp.tile(fox_q_norm_g, (1, FOX_HEADS))[:, None, :], kg=jnp.tile(fox_k_norm_g, (1, FOX_HEADS))[:, None, :],
        w_out=w_out.astype(BF16), norm2_g=norm2_g[:, None, :], w_up=w_up.astype(BF16), w_down=w_down.astype(BF16),
    )


def _gla_state_to_kernel(s):
    eye = jnp.eye(GLA_HEADS, dtype=s.dtype)
    return jnp.einsum('bhkv,hg->bhvgk', s, eye).reshape(s.shape[0], GLA_WIDTH, GLA_KW)


def _gla_state_from_kernel(st):
    nb = st.shape[0]
    s5d = st.reshape(nb, GLA_HEADS, GLA_DV, GLA_HEADS, GLA_DK)
    diag = jnp.stack([s5d[:, h, :, h, :] for h in range(GLA_HEADS)], axis=1)
    return jnp.swapaxes(diag, 2, 3)


def _tiles(nb, seq):
    n = nb * seq
    fox_tq = min(256, seq)
    return dict(in_rows=min(1024, n), out_rows=min(512, n), s5_tl=min(64, seq), gla_tl=min(1024, seq),
                gla_chunk=min(GLA_CHUNK, seq), fox_tq=fox_tq, fox_tk=max(fox_tq, LANES))


def _layer(x, p, l, s5_h0r, s5_h0i, gla_s0, past_k, past_v, past_logf):
    nb, seq, _ = x.shape
    n = nb * seq
    t = _tiles(nb, seq)
    s5_tl, gla_tl, gla_chunk, fox_tq, fox_tk = t['s5_tl'], t['gla_tl'], t['gla_chunk'], t['fox_tq'], t['fox_tk']
    x2d = x.reshape(n, D_MODEL)
    u, qk, gv, gg, misc, fq, fk, fv, logf = _in_proj(x2d, p['norm1_g'][l], p['w_pack'][l], p['fb_row'][l],
                                                     t['in_rows'])

    u_t = jnp.swapaxes(u.reshape(nb, seq, S5_WIDTH), 0, 1)
    o_s5_t, hr, hi = _s5(u_t, s5_h0r, s5_h0i, p['abar_re'][l], p['abar_im'][l], p['bmat'][l], p['cmat'][l],
                         p['s5_d'][l], p['glu_w'][l], p['glu_b'][l], s5_tl)
    o_s5 = jnp.swapaxes(o_s5_t, 0, 1).reshape(n, S5_WIDTH)

    b3 = lambda a: a.reshape(nb, seq, a.shape[-1])
    o_gla, s_t = _gla(b3(qk), b3(gv), b3(gg), b3(misc), p['a2p'][l], p['ab'][l], p['ng'][l],
                      _gla_state_to_kernel(gla_s0), gla_tl, gla_chunk)

    if past_k is None:
        o_fox, k_rows = _fox(b3(fq), b3(fk), b3(fv), b3(logf), p['qg'][l], p['kg'][l], fox_tq, fox_tk)
    else:
        plen = past_k.shape[1]
        o_fox, k_rows = _fox(b3(fq), b3(fk), b3(fv), b3(logf), p['qg'][l], p['kg'][l], fox_tq, fox_tk,
                             past_k.reshape(nb, plen, FOX_WIDTH), past_v.reshape(nb, plen, FOX_WIDTH), past_logf)

    y = _out_mlp(x2d, o_s5, o_gla.reshape(n, GLA_WIDTH), o_fox.reshape(n, FOX_WIDTH),
                 p['w_out'][l], p['norm2_g'][l], p['w_up'][l], p['w_down'][l], t['out_rows'])
    states = (hr.reshape(nb, S5_GROUPS, S5_STATE), hi.reshape(nb, S5_GROUPS, S5_STATE),
              _gla_state_from_kernel(s_t),
              k_rows.reshape(nb, seq, FOX_HEADS, FOX_HD), fv.reshape(nb, seq, FOX_HEADS, FOX_HD),
              logf.reshape(nb, seq, FOX_HEADS))
    return y.reshape(nb, seq, D_MODEL), states


def kernel(x_prompt, x_sample, state_s5_re, state_s5_im, state_gla, cache_fox_k, cache_fox_v, cache_fox_logf, norm1_g, w_in, s5_a_re, s5_a_im, s5_log_dt, s5_b_re, s5_b_im, s5_c_re, s5_c_im, s5_d, s5_glu_w, s5_glu_b, gla_a2, gla_a_bias, gla_norm_g, fox_q_norm_g, fox_k_norm_g, fox_f_bias, w_out, norm2_g, w_up, w_down):
    depth = w_in.shape[0]
    p = _pack_params(norm1_g, w_in, s5_a_re, s5_a_im, s5_log_dt, s5_b_re, s5_b_im, s5_c_re, s5_c_im, s5_d,
                     s5_glu_w, s5_glu_b, gla_a2, gla_a_bias, gla_norm_g, fox_q_norm_g, fox_k_norm_g, fox_f_bias,
                     w_out, norm2_g, w_up, w_down)
    bp = x_prompt.shape[0]
    bs = x_sample.shape[0]

    yp, ys = x_prompt, x_sample
    p_st = [[] for _ in range(6)]
    s_st = [[] for _ in range(6)]
    z_s5 = jnp.zeros((bp, S5_GN), F32)
    z_gla = jnp.zeros((bp, GLA_HEADS, GLA_DK, GLA_DV), F32)
    for l in range(depth):
        yp, stp = _layer(yp, p, l, z_s5, z_s5, z_gla, None, None, None)
        ys, sts = _layer(ys, p, l, state_s5_re[l].reshape(bs, S5_GN), state_s5_im[l].reshape(bs, S5_GN),
                         state_gla[l], cache_fox_k[l], cache_fox_v[l], cache_fox_logf[l])
        for j in range(6):
            p_st[j].append(stp[j])
            s_st[j].append(sts[j])
    p_out = [jnp.stack(a) for a in p_st]
    s_out = [jnp.stack(a) for a in s_st]
    return (yp, ys, *p_out, *s_out)
```

```python
import functools
import math

import jax
import jax.numpy as jnp
import numpy as np
from jax import lax
from jax.experimental import pallas as pl
from jax.experimental.pallas import tpu as pltpu

F32 = jnp.float32
BF16 = jnp.bfloat16

D_MODEL = 1024
EPS = 1e-6
LANES = 128

S5_WIDTH = 256
S5_GROUPS = 16
S5_GROUP = 16
S5_STATE = 64
S5_GN = S5_GROUPS * S5_STATE
S5_LANE_CHUNK = 256

GLA_WIDTH = 256
GLA_HEADS = 4
GLA_DK = 32
GLA_DV = 64
GLA_KW = 128
GLA_RANK = 16
GLA_TAU = 16.0
GLA_CHUNK = 64
GLA_CUMSUM_SPAN = 256

FOX_WIDTH = 512
FOX_HD = 64
FOX_HEADS = 8
NEG_BIG = -1e30
LOG2E = math.log2(math.e)

D_FF = 4096
FF_CHUNK = 1024

COL_U = 0
COL_QK = 256
COL_GV = 512
COL_GG = 768
COL_MISC = 1024
COL_FQ = 1152
COL_FK = 1664
COL_FV = 2176
N_PACK = 2688
MISC_FF = 0
MISC_GA = 8

VMEM_LIMIT = 56 * 1024 * 1024


def _cparams(sem):
    return pltpu.CompilerParams(dimension_semantics=sem, vmem_limit_bytes=VMEM_LIMIT)


def _const_spec(shape):
    nd = len(shape)
    return pl.BlockSpec(shape, lambda *_: (0,) * nd)


def _log_sigmoid(x):
    return jnp.minimum(x, 0.0) - jnp.log1p(jnp.exp(-jnp.abs(x)))


def _sigmoid(x):
    return 1.0 / (1.0 + jnp.exp(-x))


def _split2(x):
    hi = x.astype(BF16)
    lo = (x - hi.astype(F32)).astype(BF16)
    return hi, lo


def _split3(x):
    hi = x.astype(BF16)
    r = x - hi.astype(F32)
    mid = r.astype(BF16)
    lo = (r - mid.astype(F32)).astype(BF16)
    return hi, mid, lo


def _dot(a, b):
    return jnp.dot(a, b, preferred_element_type=F32)


def _dot_nt(a, b):
    return lax.dot_general(a, b, (((1,), (1,)), ((), ())), preferred_element_type=F32)


def _dot_tn(a, b):
    return lax.dot_general(a, b, (((0,), (0,)), ((), ())), preferred_element_type=F32)


def _segment_rms_scale(x, seg_ref, width):
    hi, lo = _split2(x * x)
    seg = seg_ref[...]
    ms = (_dot(hi, seg) + _dot(lo, seg)) * (1.0 / width)
    return lax.rsqrt(ms + EPS)


def _in_proj_kernel(x_ref, g_ref, w_ref, fb_row_ref,
                    u_ref, qk_ref, gv_ref, gg_ref, misc_ref, fq_ref, fk_ref, fv_ref, logf_ref):
    x = x_ref[...]
    ms = jnp.mean(x * x, axis=-1, keepdims=True)
    h = (x * lax.rsqrt(ms + EPS) * g_ref[...]).astype(BF16)
    for ref, c0 in ((u_ref, COL_U), (qk_ref, COL_QK), (gv_ref, COL_GV), (gg_ref, COL_GG),
                    (fq_ref, COL_FQ), (fk_ref, COL_FK), (fv_ref, COL_FV)):
        ref[...] = _dot(h, w_ref[:, c0:c0 + ref.shape[1]])
    misc = _dot(h, w_ref[:, COL_MISC:COL_MISC + LANES])
    misc_ref[...] = misc
    logf_ref[...] = _log_sigmoid(misc[:, MISC_FF:MISC_FF + FOX_HEADS] + fb_row_ref[...])


def _in_proj(x2d, g, w_pack, fb_row, tm):
    n = x2d.shape[0]
    row = lambda w: pl.BlockSpec((tm, w), lambda i: (i, 0))
    widths = (S5_WIDTH, 256, GLA_WIDTH, GLA_WIDTH, LANES, FOX_WIDTH, FOX_WIDTH, FOX_WIDTH, FOX_HEADS)
    return pl.pallas_call(
        _in_proj_kernel,
        grid=(n // tm,),
        in_specs=[row(D_MODEL), _const_spec((1, D_MODEL)), _const_spec((D_MODEL, N_PACK)),
                  _const_spec((1, FOX_HEADS))],
        out_specs=[row(w) for w in widths],
        out_shape=[jax.ShapeDtypeStruct((n, w), F32) for w in widths],
        compiler_params=_cparams(("parallel",)),
        name="in_proj",
    )(x2d, g, w_pack, fb_row)


def _gelu_tanh(x):
    c = math.sqrt(2.0 / math.pi)
    return x * (0.5 * (1.0 + jnp.tanh(c * (x + 0.044715 * (x * x * x)))))


def _s5_kernel(u_ref, h0r_ref, h0i_ref, ar_ref, ai_ref, bmat_ref, cmat_ref, d_ref, gw_ref, gb_ref,
               o_ref, hr_out_ref, hi_out_ref, hbuf, hr_s, hi_s, *, tl, nb):
    i = pl.program_id(0)

    @pl.when(i == 0)
    def _():
        hr_s[...] = h0r_ref[...]
        hi_s[...] = h0i_ref[...]

    rows = tl * nb
    u = u_ref[...].reshape(rows, S5_WIDTH)
    hbuf[...] = _dot(u.astype(BF16), bmat_ref[...])

    for c in range(S5_GN // S5_LANE_CHUNK):
        lo = c * S5_LANE_CHUNK
        re = slice(lo, lo + S5_LANE_CHUNK)
        im = slice(S5_GN + lo, S5_GN + lo + S5_LANE_CHUNK)
        ar = jnp.broadcast_to(ar_ref[:, re], (nb, S5_LANE_CHUNK))
        ai = jnp.broadcast_to(ai_ref[:, re], (nb, S5_LANE_CHUNK))

        def body(t, carry, re=re, im=im, ar=ar, ai=ai):
            hr, hi = carry
            r = pl.ds(pl.multiple_of(t * nb, nb), nb)
            nhr = ar * hr - ai * hi + hbuf[r, re]
            nhi = ar * hi + ai * hr + hbuf[r, im]
            hbuf[r, re] = nhr
            hbuf[r, im] = nhi
            return nhr, nhi

        hr, hi = lax.fori_loop(0, tl, body, (hr_s[:, re], hi_s[:, re]), unroll=min(tl, 8))
        hr_s[:, re] = hr
        hi_s[:, re] = hi

    y = _dot(hbuf[...].astype(BF16), cmat_ref[...]) + d_ref[...] * u
    y = _gelu_tanh(y)
    z = _dot(y.astype(BF16), gw_ref[...]) + gb_ref[...]
    o_ref[...] = (y * _sigmoid(z)).reshape(tl, nb, S5_WIDTH)
    hr_out_ref[...] = hr_s[...]
    hi_out_ref[...] = hi_s[...]


def _s5(u_t, h0r, h0i, ar, ai, bmat, cmat, d, gw, gb, tl):
    seq, nb, _ = u_t.shape
    kern = functools.partial(_s5_kernel, tl=tl, nb=nb)
    state = _const_spec((nb, S5_GN))
    return pl.pallas_call(
        kern,
        grid=(seq // tl,),
        in_specs=[pl.BlockSpec((tl, nb, S5_WIDTH), lambda i: (i, 0, 0)), state, state,
                  _const_spec((1, S5_GN)), _const_spec((1, S5_GN)),
                  _const_spec((S5_WIDTH, 2 * S5_GN)), _const_spec((2 * S5_GN, S5_WIDTH)),
                  _const_spec((1, S5_WIDTH)), _const_spec((S5_WIDTH, S5_WIDTH)), _const_spec((1, S5_WIDTH))],
        out_specs=[pl.BlockSpec((tl, nb, S5_WIDTH), lambda i: (i, 0, 0)), state, state],
        out_shape=[jax.ShapeDtypeStruct((seq, nb, S5_WIDTH), F32),
                   jax.ShapeDtypeStruct((nb, S5_GN), F32), jax.ShapeDtypeStruct((nb, S5_GN), F32)],
        scratch_shapes=[pltpu.VMEM((tl * nb, 2 * S5_GN), F32),
                        pltpu.VMEM((nb, S5_GN), F32), pltpu.VMEM((nb, S5_GN), F32)],
        compiler_params=_cparams(("arbitrary",)),
        name="s5",
    )(u_t, h0r, h0i, ar, ai, bmat, cmat, d, gw, gb)


def _gla_kernel(qk_ref, v_ref, g_ref, misc_ref, a2_ref, ab_ref, ng_ref, s0_ref,
                tril_ref, mk_ref, mv_ref, ms_ref, causal_ref, seg_ref,
                o_ref, sfin_ref, s_scr, *, tl, chunk):
    i = pl.program_id(1)

    @pl.when(i == 0)
    def _():
        s_scr[...] = s0_ref[0]

    qk = qk_ref[0]
    q = qk[:, :GLA_KW] * (GLA_DK ** -0.5)
    k = qk[:, GLA_KW:]
    v = v_ref[0]
    la = _log_sigmoid(_dot(misc_ref[0].astype(BF16), a2_ref[...]) + ab_ref[...]) * (1.0 / GLA_TAU)

    tril = tril_ref[...]
    span = tril.shape[0]
    parts = _split3(la)
    b = jnp.concatenate([sum(_dot(tril, part[r0:r0 + span]) for part in parts) for r0 in range(0, tl, span)], axis=0)
    nchunk = tl // chunk
    b3 = b.reshape(nchunk, chunk, GLA_KW)
    b_last = b3[:, chunk - 1:chunk, :]
    b_end = (b_last - b3).reshape(tl, GLA_KW)
    q_dec = (q * jnp.exp(b)).astype(BF16)
    k_inv = k * jnp.exp(-b)
    k_end = (k * jnp.exp(b_end)).astype(BF16)
    e_last = jnp.exp(b_last)

    outs = []
    for c in range(nchunk):
        sl = slice(c * chunk, (c + 1) * chunk)
        qd = q_dec[sl]
        kbd = jnp.where(mk_ref[...] > 0, jnp.concatenate([k_inv[sl]] * GLA_HEADS, axis=0), 0.0).astype(BF16)
        att = jnp.where(causal_ref[...] > 0, _dot_nt(qd, kbd), 0.0)
        vc = v[sl]
        vbd = jnp.where(mv_ref[...] > 0, jnp.concatenate([vc] * GLA_HEADS, axis=0), 0.0).astype(BF16)
        s = s_scr[...]
        outs.append(_dot(att.astype(BF16), vbd) + _dot_nt(qd, s.astype(BF16)))
        upd = jnp.where(ms_ref[...] > 0, _dot_tn(vc.astype(BF16), k_end[sl]), 0.0)
        s_scr[...] = s * e_last[c] + upd
    o = outs[0] if nchunk == 1 else jnp.concatenate(outs, axis=0)

    o = o * _segment_rms_scale(o, seg_ref, GLA_DV) * ng_ref[...]
    g = g_ref[0]
    o_ref[0] = o * (g * _sigmoid(g))
    sfin_ref[0] = s_scr[...]


def _gla_masks(tl, chunk):
    t = np.arange(min(tl, GLA_CUMSUM_SPAN))
    tril = ((t[:, None] // chunk == t[None, :] // chunk) & (t[None, :] <= t[:, None])).astype(np.float32)
    r = np.arange(GLA_HEADS * chunk)
    mk = (r[:, None] // chunk == np.arange(GLA_KW)[None, :] // GLA_DK).astype(np.float32)
    mv = (r[:, None] // chunk == np.arange(GLA_WIDTH)[None, :] // GLA_DV).astype(np.float32)
    ms = (np.arange(GLA_WIDTH)[:, None] // GLA_DV == np.arange(GLA_KW)[None, :] // GLA_DK).astype(np.float32)
    causal = (np.arange(chunk)[:, None] >= (r[None, :] % chunk)).astype(np.float32)
    seg = (np.arange(GLA_WIDTH)[:, None] // GLA_DV == np.arange(GLA_WIDTH)[None, :] // GLA_DV).astype(np.float32)
    return (jnp.asarray(tril, BF16), jnp.asarray(mk), jnp.asarray(mv), jnp.asarray(ms), jnp.asarray(causal),
            jnp.asarray(seg, BF16))


def _gla(qk, v, g, misc, a2p, ab, ng, s0t, tl, chunk):
    nb, seq, _ = qk.shape
    kern = functools.partial(_gla_kernel, tl=tl, chunk=chunk)
    masks = _gla_masks(tl, chunk)
    tok = lambda w: pl.BlockSpec((1, tl, w), lambda b, i: (b, i, 0))
    st = pl.BlockSpec((1, GLA_WIDTH, GLA_KW), lambda b, i: (b, 0, 0))
    return pl.pallas_call(
        kern,
        grid=(nb, seq // tl),
        in_specs=[tok(256), tok(GLA_WIDTH), tok(GLA_WIDTH), tok(LANES),
                  _const_spec((LANES, GLA_KW)), _const_spec((1, GLA_KW)), _const_spec((1, GLA_WIDTH)), st]
                 + [_const_spec(m.shape) for m in masks],
        out_specs=[tok(GLA_WIDTH), st],
        out_shape=[jax.ShapeDtypeStruct((nb, seq, GLA_WIDTH), F32),
                   jax.ShapeDtypeStruct((nb, GLA_WIDTH, GLA_KW), F32)],
        scratch_shapes=[pltpu.VMEM((GLA_WIDTH, GLA_KW), F32)],
        compiler_params=_cparams(("parallel", "arbitrary")),
        name="gla",
    )(qk, v, g, misc, a2p, ab, ng, s0t, *masks)


FOX_KAUG = 2 * FOX_HD


def _fox_constants():
    sel_k = np.zeros((FOX_HEADS // 2, 2 * LANES, 2 * FOX_KAUG), np.float32)
    sel_q = np.zeros((2 * FOX_KAUG, LANES), np.float32)
    ones_q = np.zeros((2 * FOX_KAUG, 1), np.float32)
    for hh in range(2):
        for d in range(FOX_HD):
            sel_k[:, hh * FOX_HD + d, hh * FOX_KAUG + d] = 1.0
            sel_q[hh * FOX_KAUG + d, hh * FOX_HD + d] = 1.0
        for part in range(3):
            ones_q[hh * FOX_KAUG + FOX_HD + part, 0] = 1.0
            for pair in range(FOX_HEADS // 2):
                sel_k[pair, LANES + part * FOX_HEADS + 2 * pair + hh, hh * FOX_KAUG + FOX_HD + part] = 1.0
    eye = np.eye(FOX_WIDTH, dtype=np.float32)
    t = np.arange(FOX_WIDTH)
    seg = (t[:, None] // FOX_HD == t[None, :] // FOX_HD).astype(np.float32)
    return (jnp.asarray(sel_k, BF16), jnp.asarray(sel_q, BF16), jnp.asarray(ones_q), jnp.asarray(eye, BF16),
            jnp.asarray(seg, BF16))


def _lower_ones(n):
    t = np.arange(n)
    return jnp.asarray((t[None, :] <= t[:, None]).astype(np.float32), BF16)


def _fox_kernel(*refs, tq, tqp, tk, seq, past, tkp):
    if past:
        (fq_ref, fk_ref, fv_ref, lf_ref, qg_ref, kg_ref, seg_ref, selk_ref, selq_ref, onesq_ref, eye_ref, tril_ref,
         pk_ref, pv_ref, plf_ref, trilp_ref, o_ref, krow_ref, kaug, vt, s_a, s_b, m_s, l_s, acc_s, pkaug, pvt) = refs
    else:
        (fq_ref, fk_ref, fv_ref, lf_ref, qg_ref, kg_ref, seg_ref, selk_ref, selq_ref, onesq_ref, eye_ref, tril_ref,
         o_ref, krow_ref, kaug, vt, s_a, s_b, m_s, l_s, acc_s) = refs
    i = pl.program_id(1)
    blk = min(tk, seq)

    def stage_block(k_bf, v_bf, logf, carry, tril, kaug_dst, vt_dst):
        n = logf.shape[0]
        logf128 = jnp.concatenate([logf, jnp.zeros((n, LANES - FOX_HEADS), F32)], axis=1)
        c = sum(_dot(tril, part) for part in _split3(logf128)) + carry
        hi, mid, lo = (part.astype(F32) for part in _split3(c * (-LOG2E)))
        packed = (hi + pltpu.roll(mid, FOX_HEADS, 1) + pltpu.roll(lo, 2 * FOX_HEADS, 1)).astype(BF16)
        for pair in range(FOX_HEADS // 2):
            x = jnp.concatenate([k_bf[:, pair * LANES:(pair + 1) * LANES], packed], axis=1)
            kaug_dst[0:n, pair * 2 * FOX_KAUG:(pair + 1) * 2 * FOX_KAUG] = _dot(x, selk_ref[pair]).astype(BF16)
        vt_dst[:, 0:n] = _dot_nt(eye_ref[...], v_bf).astype(BF16)
        return c[n - 1:n, :]

    @pl.when(i == 0)
    def _prologue():
        carry = jnp.zeros((1, LANES), F32)
        if past:
            for jb in range(past // tkp):
                r = slice(jb * tkp, (jb + 1) * tkp)
                carry = stage_block(pk_ref[0, r, :].astype(BF16), pv_ref[0, r, :].astype(BF16), plf_ref[0, r, :],
                                    carry, trilp_ref[...], pkaug.at[jb], pvt.at[jb])
        if blk < tk:
            kaug[...] = jnp.zeros_like(kaug)
            vt[...] = jnp.zeros_like(vt)
        for jb in range(max(seq // tk, 1)):
            r = slice(jb * tk, jb * tk + blk)
            kf = fk_ref[0, r, :]
            kn = kf * _segment_rms_scale(kf, seg_ref, FOX_HD) * kg_ref[...]
            krow_ref[0, r, :] = kn
            carry = stage_block(kn.astype(BF16), fv_ref[0, r, :].astype(BF16), lf_ref[0, r, :],
                                carry, tril_ref[...], kaug.at[jb], vt.at[jb])

    qf = fq_ref[0]
    qn = (qf * _segment_rms_scale(qf, seg_ref, FOX_HD) * qg_ref[...] * (FOX_HD ** -0.5 * LOG2E)).astype(BF16)
    if tqp > tq:
        qn = jnp.concatenate([qn, jnp.zeros((tqp - tq, FOX_WIDTH), BF16)], axis=0)
    qt_all = jnp.concatenate(
        [(_dot_nt(selq_ref[...], qn[:, pair * LANES:(pair + 1) * LANES]) + onesq_ref[...]).astype(BF16)
         for pair in range(FOX_HEADS // 2)], axis=0)
    key_i = lax.broadcasted_iota(jnp.int32, (tk, tqp), 0)
    qry_i = lax.broadcasted_iota(jnp.int32, (tk, tqp), 1)
    diag_ok = key_i <= qry_i

    klanes = [slice(h * FOX_KAUG, (h + 1) * FOX_KAUG) for h in range(FOX_HEADS)]
    vrows = [slice(h * FOX_HD, (h + 1) * FOX_HD) for h in range(FOX_HEADS)]
    qts = [qt_all[klanes[h], :] for h in range(FOX_HEADS)]

    def softmax_pv(h, s, v_t, mask):
        if mask is not None:
            s = jnp.where(mask, s, NEG_BIG)
        m = m_s[h]
        m_new = jnp.maximum(m, jnp.max(s, axis=0, keepdims=True))
        p = jnp.exp2(s - m_new)
        alpha = jnp.exp2(m - m_new)
        m_s[h] = m_new
        l_s[h] = alpha * l_s[h] + jnp.sum(p, axis=0, keepdims=True)
        acc_s[h] = alpha * acc_s[h] + _dot(v_t, p.astype(BF16))

    def block_step(j, src, dst):
        for h in range(FOX_HEADS):
            dst[h] = _dot(kaug[j + 1, :, klanes[h]], qts[h])
            softmax_pv(h, src[h], vt[j, vrows[h], :], None)

    for h in range(FOX_HEADS):
        m_s[h] = jnp.full((1, tqp), NEG_BIG, F32)
        l_s[h] = jnp.zeros((1, tqp), F32)
        acc_s[h] = jnp.zeros((FOX_HD, tqp), F32)
        if past:
            for jb in range(past // tkp):
                softmax_pv(h, _dot(pkaug[jb, :, klanes[h]], qts[h]), pvt[jb, vrows[h], :], None)
        s_a[h] = _dot(kaug[0, :, klanes[h]], qts[h])

    def body(jj, carry):
        block_step(2 * jj, s_a, s_b)
        block_step(2 * jj + 1, s_b, s_a)
        return carry

    lax.fori_loop(0, i // 2, body, 0)

    @pl.when(i % 2 == 1)
    def _odd_tail():
        block_step(i - 1, s_a, s_b)
        s_a[...] = s_b[...]

    outs = []
    for h in range(FOX_HEADS):
        softmax_pv(h, s_a[h], vt[i, vrows[h], :], diag_ok)
        outs.append(acc_s[h] / l_s[h])
    o_t = jnp.concatenate(outs, axis=0)
    o_ref[0] = o_t.T[0:tq, :]


def _fox(fq, fk, fv, logf, qg, kg, tq, tk, past_k=None, past_v=None, past_logf=None):
    nb, seq, _ = fq.shape
    nq = seq // tq
    tqp = max(tq, LANES)
    assert tqp == tk and (tq == tk or nq == 1)
    nkb = max(seq // tk, 1)
    past = 0 if past_k is None else past_k.shape[1]
    tkp = min(past, 256) if past else 0
    sel_k, sel_q, ones_q, eye, seg = _fox_constants()
    kern = functools.partial(_fox_kernel, tq=tq, tqp=tqp, tk=tk, seq=seq, past=past, tkp=tkp)
    full = lambda rows, w: pl.BlockSpec((1, rows, w), lambda b, i: (b, 0, 0))
    consts = [qg, kg, seg, sel_k, sel_q, ones_q, eye, _lower_ones(min(tk, seq))]
    in_specs = [pl.BlockSpec((1, tq, FOX_WIDTH), lambda b, i: (b, i, 0)), full(seq, FOX_WIDTH), full(seq, FOX_WIDTH),
                full(seq, FOX_HEADS)] + [_const_spec(c.shape) for c in consts]
    args = [fq, fk, fv, logf] + consts
    scratch = [pltpu.VMEM((nkb, tk, FOX_HEADS * FOX_KAUG), BF16), pltpu.VMEM((nkb, FOX_WIDTH, tk), BF16),
               pltpu.VMEM((FOX_HEADS, tk, tqp), F32), pltpu.VMEM((FOX_HEADS, tk, tqp), F32),
               pltpu.VMEM((FOX_HEADS, 1, tqp), F32), pltpu.VMEM((FOX_HEADS, 1, tqp), F32),
               pltpu.VMEM((FOX_HEADS, FOX_HD, tqp), F32)]
    if past:
        in_specs += [full(past, FOX_WIDTH), full(past, FOX_WIDTH), full(past, FOX_HEADS), _const_spec((tkp, tkp))]
        args += [past_k, past_v, past_logf, _lower_ones(tkp)]
        scratch += [pltpu.VMEM((past // tkp, tkp, FOX_HEADS * FOX_KAUG), BF16),
                    pltpu.VMEM((past // tkp, FOX_WIDTH, tkp), BF16)]
    return pl.pallas_call(
        kern,
        grid=(nb, nq),
        in_specs=in_specs,
        out_specs=[pl.BlockSpec((1, tq, FOX_WIDTH), lambda b, i: (b, i, 0)), full(seq, FOX_WIDTH)],
        out_shape=[jax.ShapeDtypeStruct((nb, seq, FOX_WIDTH), F32), jax.ShapeDtypeStruct((nb, seq, FOX_WIDTH), F32)],
        scratch_shapes=scratch,
        compiler_params=_cparams(("parallel", "arbitrary")),
        name="fox",
    )(*args)


def _out_mlp_kernel(x_ref, s5_ref, gla_ref, fox_ref, wo_ref, g2_ref, wu_ref, wd_ref, y_ref, up_s):
    x1 = (x_ref[...]
          + _dot(s5_ref[...].astype(BF16), wo_ref[0:S5_WIDTH, :])
          + _dot(gla_ref[...].astype(BF16), wo_ref[S5_WIDTH:S5_WIDTH + GLA_WIDTH, :])
          + _dot(fox_ref[...].astype(BF16), wo_ref[S5_WIDTH + GLA_WIDTH:, :]))
    ms = jnp.mean(x1 * x1, axis=-1, keepdims=True)
    h2 = (x1 * lax.rsqrt(ms + EPS) * g2_ref[...]).astype(BF16)
    for c in range(D_FF // FF_CHUNK):
        cols = slice(c * FF_CHUNK, (c + 1) * FF_CHUNK)
        up = jnp.maximum(_dot(h2, wu_ref[:, cols]), 0.0)
        up_s[:, cols] = (up * up).astype(BF16)
    y_ref[...] = x1 + _dot(up_s[...], wd_ref[...])


def _out_mlp(x2d, s5, gla, fox, wo, g2, wu, wd, tm):
    n = x2d.shape[0]
    row = lambda w: pl.BlockSpec((tm, w), lambda i: (i, 0))
    resident = pl.BlockSpec(memory_space=pltpu.VMEM)
    return pl.pallas_call(
        _out_mlp_kernel,
        grid=(n // tm,),
        in_specs=[row(D_MODEL), row(S5_WIDTH), row(GLA_WIDTH), row(FOX_WIDTH),
                  resident, _const_spec((1, D_MODEL)), resident, resident],
        out_specs=row(D_MODEL),
        out_shape=jax.ShapeDtypeStruct((n, D_MODEL), F32),
        scratch_shapes=[pltpu.VMEM((tm, D_FF), BF16)],
        compiler_params=_cparams(("parallel",)),
        name="out_mlp",
    )(x2d, s5, gla, fox, wo, g2, wu, wd)


def _pack_params(norm1_g, w_in, s5_a_re, s5_a_im, s5_log_dt, s5_b_re, s5_b_im, s5_c_re, s5_c_im, s5_d,
                 s5_glu_w, s5_glu_b, gla_a2, gla_a_bias, gla_norm_g, fox_q_norm_g, fox_k_norm_g, fox_f_bias,
                 w_out, norm2_g, w_up, w_down):
    depth = w_in.shape[0]
    o_ga = 1024
    o_fq = o_ga + GLA_RANK
    o_ff = o_fq + 3 * FOX_WIDTH
    w_ff = w_in[:, :, o_ff:o_ff + FOX_HEADS]
    pad = jnp.zeros((depth, D_MODEL, LANES - FOX_HEADS - GLA_RANK), F32)
    w_pack = jnp.concatenate([w_in[:, :, :o_ga], w_ff, w_in[:, :, o_ga:o_fq], pad, w_in[:, :, o_fq:o_ff]],
                             axis=-1).astype(BF16)
    a2p = jnp.zeros((depth, LANES, GLA_KW), F32).at[:, MISC_GA:MISC_GA + GLA_RANK, :].set(gla_a2).astype(BF16)

    dt = jnp.exp(s5_log_dt)[..., None]
    mag = jnp.exp(dt * s5_a_re)
    abar_re = mag * jnp.cos(dt * s5_a_im)
    abar_im = mag * jnp.sin(dt * s5_a_im)
    den = s5_a_re * s5_a_re + s5_a_im * s5_a_im
    zr = ((abar_re - 1.0) * s5_a_re + abar_im * s5_a_im) / den
    zi = (abar_im * s5_a_re - (abar_re - 1.0) * s5_a_im) / den
    bb_re = zr[..., None] * s5_b_re - zi[..., None] * s5_b_im
    bb_im = zr[..., None] * s5_b_im + zi[..., None] * s5_b_re
    eye = jnp.eye(S5_GROUPS, dtype=F32)
    to_b = lambda bb: jnp.einsum('lgnp,gh->lgphn', bb, eye).reshape(depth, S5_WIDTH, S5_GN)
    bmat = jnp.concatenate([to_b(bb_re), to_b(bb_im)], axis=-1).astype(BF16)
    to_c = lambda cc: jnp.einsum('lgpn,gh->lgnhp', cc, eye).reshape(depth, S5_GN, S5_WIDTH)
    cmat = jnp.concatenate([to_c(s5_c_re), -to_c(s5_c_im)], axis=1).astype(BF16)

    return dict(
        norm1_g=norm1_g[:, None, :], w_pack=w_pack, fb_row=fox_f_bias[:, None, :],
        abar_re=abar_re.reshape(depth, 1, S5_GN), abar_im=abar_im.reshape(depth, 1, S5_GN),
        bmat=bmat, cmat=cmat, s5_d=s5_d.reshape(depth, 1, S5_WIDTH),
        glu_w=s5_glu_w.astype(BF16), glu_b=s5_glu_b[:, None, :],
        a2p=a2p, ab=gla_a_bias[:, None, :], ng=jnp.tile(gla_norm_g, (1, GLA_HEADS))[:, None, :],
        qg=jnp.tile(fox_q_norm_g, (1, FOX_HEADS))[:, None, :], kg=jnp.tile(fox_k_norm_g, (1, FOX_HEADS))[:, None, :],
        w_out=w_out.astype(BF16), norm2_g=norm2_g[:, None, :], w_up=w_up.astype(BF16), w_down=w_down.astype(BF16),
    )


def _gla_state_to_kernel(s):
    eye = jnp.eye(GLA_HEADS, dtype=s.dtype)
    return jnp.einsum('bhkv,hg->bhvgk', s, eye).reshape(s.shape[0], GLA_WIDTH, GLA_KW)


def _gla_state_from_kernel(st):
    nb = st.shape[0]
    s5d = st.reshape(nb, GLA_HEADS, GLA_DV, GLA_HEADS, GLA_DK)
    diag = jnp.stack([s5d[:, h, :, h, :] for h in range(GLA_HEADS)], axis=1)
    return jnp.swapaxes(diag, 2, 3)


def _tiles(nb, seq):
    n = nb * seq
    fox_tq = min(256, seq)
    return dict(in_rows=min(1024, n), out_rows=min(512, n), s5_tl=min(64, seq), gla_tl=min(1024, seq),
                gla_chunk=min(GLA_CHUNK, seq), fox_tq=fox_tq, fox_tk=max(fox_tq, LANES))


def _layer(x, p, l, s5_h0r, s5_h0i, gla_s0, past_k, past_v, past_logf):
    nb, seq, _ = x.shape
    n = nb * seq
    t = _tiles(nb, seq)
    s5_tl, gla_tl, gla_chunk, fox_tq, fox_tk = t['s5_tl'], t['gla_tl'], t['gla_chunk'], t['fox_tq'], t['fox_tk']
    x2d = x.reshape(n, D_MODEL)
    u, qk, gv, gg, misc, fq, fk, fv, logf = _in_proj(x2d, p['norm1_g'][l], p['w_pack'][l], p['fb_row'][l],
                                                     t['in_rows'])

    u_t = jnp.swapaxes(u.reshape(nb, seq, S5_WIDTH), 0, 1)
    o_s5_t, hr, hi = _s5(u_t, s5_h0r, s5_h0i, p['abar_re'][l], p['abar_im'][l], p['bmat'][l], p['cmat'][l],
                         p['s5_d'][l], p['glu_w'][l], p['glu_b'][l], s5_tl)
    o_s5 = jnp.swapaxes(o_s5_t, 0, 1).reshape(n, S5_WIDTH)

    b3 = lambda a: a.reshape(nb, seq, a.shape[-1])
    o_gla, s_t = _gla(b3(qk), b3(gv), b3(gg), b3(misc), p['a2p'][l], p['ab'][l], p['ng'][l],
                      _gla_state_to_kernel(gla_s0), gla_tl, gla_chunk)

    if past_k is None:
        o_fox, k_rows = _fox(b3(fq), b3(fk), b3(fv), b3(logf), p['qg'][l], p['kg'][l], fox_tq, fox_tk)
    else:
        plen = past_k.shape[1]
        o_fox, k_rows = _fox(b3(fq), b3(fk), b3(fv), b3(logf), p['qg'][l], p['kg'][l], fox_tq, fox_tk,
                             past_k.reshape(nb, plen, FOX_WIDTH), past_v.reshape(nb, plen, FOX_WIDTH), past_logf)

    y = _out_mlp(x2d, o_s5, o_gla.reshape(n, GLA_WIDTH), o_fox.reshape(n, FOX_WIDTH),
                 p['w_out'][l], p['norm2_g'][l], p['w_up'][l], p['w_down'][l], t['out_rows'])
    states = (hr.reshape(nb, S5_GROUPS, S5_STATE), hi.reshape(nb, S5_GROUPS, S5_STATE),
              _gla_state_from_kernel(s_t),
              k_rows.reshape(nb, seq, FOX_HEADS, FOX_HD), fv.reshape(nb, seq, FOX_HEADS, FOX_HD),
              logf.reshape(nb, seq, FOX_HEADS))
    return y.reshape(nb, seq, D_MODEL), states


def kernel(x_prompt, x_sample, state_s5_re, state_s5_im, state_gla, cache_fox_k, cache_fox_v, cache_fox_logf, norm1_g, w_in, s5_a_re, s5_a_im, s5_log_dt, s5_b_re, s5_b_im, s5_c_re, s5_c_im, s5_d, s5_glu_w, s5_glu_b, gla_a2, gla_a_bias, gla_norm_g, fox_q_norm_g, fox_k_norm_g, fox_f_bias, w_out, norm2_g, w_up, w_down):
    depth = w_in.shape[0]
    p = _pack_params(norm1_g, w_in, s5_a_re, s5_a_im, s5_log_dt, s5_b_re, s5_b_im, s5_c_re, s5_c_im, s5_d,
                     s5_glu_w, s5_glu_b, gla_a2, gla_a_bias, gla_norm_g, fox_q_norm_g, fox_k_norm_g, fox_f_bias,
                     w_out, norm2_g, w_up, w_down)
    bp = x_prompt.shape[0]
    bs = x_sample.shape[0]

    yp, ys = x_prompt, x_sample
    p_st = [[] for _ in range(6)]
    s_st = [[] for _ in range(6)]
    z_s5 = jnp.zeros((bp, S5_GN), F32)
    z_gla = jnp.zeros((bp, GLA_HEADS, GLA_DK, GLA_DV), F32)
    for l in range(depth):
        yp, stp = _layer(yp, p, l, z_s5, z_s5, z_gla, None, None, None)
        ys, sts = _layer(ys, p, l, state_s5_re[l].reshape(bs, S5_GN), state_s5_im[l].reshape(bs, S5_GN),
                         state_gla[l], cache_fox_k[l], cache_fox_v[l], cache_fox_logf[l])
        for j in range(6):
            p_st[j].append(stp[j])
            s_st[j].append(sts[j])
    p_out = [jnp.stack(a) for a in p_st]
    s_out = [jnp.stack(a) for a in s_st]
    return (yp, ys, *p_out, *s_out)
```

```python
import functools
import math

import jax
import jax.numpy as jnp
import numpy as np
from jax import lax
from jax.experimental import pallas as pl
from jax.experimental.pallas import tpu as pltpu

F32 = jnp.float32
BF16 = jnp.bfloat16

D_MODEL = 1024
EPS = 1e-6
LANES = 128

S5_WIDTH = 256
S5_GROUPS = 16
S5_GROUP = 16
S5_STATE = 64
S5_GN = S5_GROUPS * S5_STATE
S5_LANE_CHUNK = 256

GLA_WIDTH = 256
GLA_HEADS = 4
GLA_DK = 32
GLA_DV = 64
GLA_KW = 128
GLA_RANK = 16
GLA_TAU = 16.0
GLA_CHUNK = 64
GLA_CUMSUM_SPAN = 256

FOX_WIDTH = 512
FOX_HD = 64
FOX_HEADS = 8
NEG_BIG = -1e30
LOG2E = math.log2(math.e)

D_FF = 4096
FF_CHUNK = 1024

COL_U = 0
COL_QK = 256
COL_GV = 512
COL_GG = 768
COL_MISC = 1024
COL_FQ = 1152
COL_FK = 1664
COL_FV = 2176
N_PACK = 2688
MISC_FF = 0
MISC_GA = 8

VMEM_LIMIT = 56 * 1024 * 1024


def _cparams(sem):
    return pltpu.CompilerParams(dimension_semantics=sem, vmem_limit_bytes=VMEM_LIMIT)


def _const_spec(shape):
    nd = len(shape)
    return pl.BlockSpec(shape, lambda *_: (0,) * nd)


def _log_sigmoid(x):
    return jnp.minimum(x, 0.0) - jnp.log1p(jnp.exp(-jnp.abs(x)))


def _sigmoid(x):
    return 1.0 / (1.0 + jnp.exp(-x))


def _split2(x):
    hi = x.astype(BF16)
    lo = (x - hi.astype(F32)).astype(BF16)
    return hi, lo


def _split3(x):
    hi = x.astype(BF16)
    r = x - hi.astype(F32)
    mid = r.astype(BF16)
    lo = (r - mid.astype(F32)).astype(BF16)
    return hi, mid, lo


def _dot(a, b):
    return jnp.dot(a, b, preferred_element_type=F32)


def _dot_nt(a, b):
    return lax.dot_general(a, b, (((1,), (1,)), ((), ())), preferred_element_type=F32)


def _dot_tn(a, b):
    return lax.dot_general(a, b, (((0,), (0,)), ((), ())), preferred_element_type=F32)


def _segment_rms_scale(x, seg_ref, width):
    hi, lo = _split2(x * x)
    seg = seg_ref[...]
    ms = (_dot(hi, seg) + _dot(lo, seg)) * (1.0 / width)
    return lax.rsqrt(ms + EPS)


def _in_proj_kernel(x_ref, g_ref, w_ref, fb_row_ref, *refs):
    u_ref, qk_ref, gv_ref, gg_ref, misc_ref, fq_ref, fk_ref, fv_ref, logf_ref = refs[2:]
    x = x_ref[...]
    ms = jnp.mean(x * x, axis=-1, keepdims=True)
    h = (x * lax.rsqrt(ms + EPS) * g_ref[...]).astype(BF16)
    for ref, c0 in ((u_ref, COL_U), (qk_ref, COL_QK), (gv_ref, COL_GV), (gg_ref, COL_GG),
                    (fq_ref, COL_FQ), (fk_ref, COL_FK), (fv_ref, COL_FV)):
        ref[...] = _dot(h, w_ref[:, c0:c0 + ref.shape[1]])
    misc = _dot(h, w_ref[:, COL_MISC:COL_MISC + LANES])
    misc_ref[...] = misc
    logf_ref[...] = _log_sigmoid(misc[:, MISC_FF:MISC_FF + FOX_HEADS] + fb_row_ref[...])


def _in_proj(x2d, g, w_pack, fb_row, tm, layer, depth, stacks):
    n = x2d.shape[0]
    row = lambda w: pl.BlockSpec((tm, w), lambda i: (i, 0))
    slab = lambda w: pl.BlockSpec((None, tm, w), lambda i: (layer, i, 0))
    widths = (S5_WIDTH, 256, GLA_WIDTH, GLA_WIDTH, LANES, FOX_WIDTH, FOX_WIDTH)
    in_specs = [row(D_MODEL), _const_spec((1, D_MODEL)), _const_spec((D_MODEL, N_PACK)), _const_spec((1, FOX_HEADS))]
    args = [x2d, g, w_pack, fb_row]
    aliases = {len(args): len(widths), len(args) + 1: len(widths) + 1}
    in_specs += [pl.BlockSpec(memory_space=pl.ANY)] * 2
    args += list(stacks)
    return pl.pallas_call(
        _in_proj_kernel,
        grid=(n // tm,),
        in_specs=in_specs,
        out_specs=[row(w) for w in widths] + [slab(FOX_WIDTH), slab(FOX_HEADS)],
        out_shape=[jax.ShapeDtypeStruct((n, w), F32) for w in widths]
                  + [jax.ShapeDtypeStruct((depth, n, FOX_WIDTH), F32), jax.ShapeDtypeStruct((depth, n, FOX_HEADS), F32)],
        input_output_aliases=aliases,
        compiler_params=_cparams(("parallel",)),
        name="in_proj",
    )(*args)


def _gelu_tanh(x):
    c = math.sqrt(2.0 / math.pi)
    return x * (0.5 * (1.0 + jnp.tanh(c * (x + 0.044715 * (x * x * x)))))


def _s5_kernel(u_ref, h0r_ref, h0i_ref, ar_ref, ai_ref, bmat_ref, cmat_ref, d_ref, gw_ref, gb_ref,
               o_ref, hr_out_ref, hi_out_ref, hbuf, hr_s, hi_s, *, tl, nb):
    i = pl.program_id(0)

    @pl.when(i == 0)
    def _():
        hr_s[...] = h0r_ref[...]
        hi_s[...] = h0i_ref[...]

    rows = tl * nb
    u = u_ref[...].reshape(rows, S5_WIDTH)
    hbuf[...] = _dot(u.astype(BF16), bmat_ref[...])

    for c in range(S5_GN // S5_LANE_CHUNK):
        lo = c * S5_LANE_CHUNK
        re = slice(lo, lo + S5_LANE_CHUNK)
        im = slice(S5_GN + lo, S5_GN + lo + S5_LANE_CHUNK)
        ar = jnp.broadcast_to(ar_ref[:, re], (nb, S5_LANE_CHUNK))
        ai = jnp.broadcast_to(ai_ref[:, re], (nb, S5_LANE_CHUNK))

        def body(t, carry, re=re, im=im, ar=ar, ai=ai):
            hr, hi = carry
            r = pl.ds(pl.multiple_of(t * nb, nb), nb)
            nhr = ar * hr - ai * hi + hbuf[r, re]
            nhi = ar * hi + ai * hr + hbuf[r, im]
            hbuf[r, re] = nhr
            hbuf[r, im] = nhi
            return nhr, nhi

        hr, hi = lax.fori_loop(0, tl, body, (hr_s[:, re], hi_s[:, re]), unroll=min(tl, 8))
        hr_s[:, re] = hr
        hi_s[:, re] = hi

    y = _dot(hbuf[...].astype(BF16), cmat_ref[...]) + d_ref[...] * u
    y = _gelu_tanh(y)
    z = _dot(y.astype(BF16), gw_ref[...]) + gb_ref[...]
    o_ref[...] = (y * _sigmoid(z)).reshape(tl, nb, S5_WIDTH)
    hr_out_ref[...] = hr_s[...]
    hi_out_ref[...] = hi_s[...]


def _s5(u_t, h0r, h0i, ar, ai, bmat, cmat, d, gw, gb, tl):
    seq, nb, _ = u_t.shape
    kern = functools.partial(_s5_kernel, tl=tl, nb=nb)
    state = _const_spec((nb, S5_GN))
    return pl.pallas_call(
        kern,
        grid=(seq // tl,),
        in_specs=[pl.BlockSpec((tl, nb, S5_WIDTH), lambda i: (i, 0, 0)), state, state,
                  _const_spec((1, S5_GN)), _const_spec((1, S5_GN)),
                  _const_spec((S5_WIDTH, 2 * S5_GN)), _const_spec((2 * S5_GN, S5_WIDTH)),
                  _const_spec((1, S5_WIDTH)), _const_spec((S5_WIDTH, S5_WIDTH)), _const_spec((1, S5_WIDTH))],
        out_specs=[pl.BlockSpec((tl, nb, S5_WIDTH), lambda i: (i, 0, 0)), state, state],
        out_shape=[jax.ShapeDtypeStruct((seq, nb, S5_WIDTH), F32),
                   jax.ShapeDtypeStruct((nb, S5_GN), F32), jax.ShapeDtypeStruct((nb, S5_GN), F32)],
        scratch_shapes=[pltpu.VMEM((tl * nb, 2 * S5_GN), F32),
                        pltpu.VMEM((nb, S5_GN), F32), pltpu.VMEM((nb, S5_GN), F32)],
        compiler_params=_cparams(("arbitrary",)),
        name="s5",
    )(u_t, h0r, h0i, ar, ai, bmat, cmat, d, gw, gb)


def _gla_kernel(qk_ref, v_ref, g_ref, misc_ref, a2_ref, ab_ref, ng_ref, s0_ref,
                tril_ref, mk_ref, mv_ref, ms_ref, causal_ref, seg_ref,
                o_ref, sfin_ref, s_scr, *, tl, chunk):
    i = pl.program_id(1)

    @pl.when(i == 0)
    def _():
        s_scr[...] = s0_ref[0]

    qk = qk_ref[0]
    q = qk[:, :GLA_KW] * (GLA_DK ** -0.5)
    k = qk[:, GLA_KW:]
    v = v_ref[0]
    la = _log_sigmoid(_dot(misc_ref[0].astype(BF16), a2_ref[...]) + ab_ref[...]) * (1.0 / GLA_TAU)

    tril = tril_ref[...]
    span = tril.shape[0]
    parts = _split3(la)
    b = jnp.concatenate([sum(_dot(tril, part[r0:r0 + span]) for part in parts) for r0 in range(0, tl, span)], axis=0)
    nchunk = tl // chunk
    b3 = b.reshape(nchunk, chunk, GLA_KW)
    b_last = b3[:, chunk - 1:chunk, :]
    b_end = (b_last - b3).reshape(tl, GLA_KW)
    q_dec = (q * jnp.exp(b)).astype(BF16)
    k_inv = k * jnp.exp(-b)
    k_end = (k * jnp.exp(b_end)).astype(BF16)
    e_last = jnp.exp(b_last)

    outs = []
    for c in range(nchunk):
        sl = slice(c * chunk, (c + 1) * chunk)
        qd = q_dec[sl]
        kbd = jnp.where(mk_ref[...] > 0, jnp.concatenate([k_inv[sl]] * GLA_HEADS, axis=0), 0.0).astype(BF16)
        att = jnp.where(causal_ref[...] > 0, _dot_nt(qd, kbd), 0.0)
        vc = v[sl]
        vbd = jnp.where(mv_ref[...] > 0, jnp.concatenate([vc] * GLA_HEADS, axis=0), 0.0).astype(BF16)
        s = s_scr[...]
        outs.append(_dot(att.astype(BF16), vbd) + _dot_nt(qd, s.astype(BF16)))
        upd = jnp.where(ms_ref[...] > 0, _dot_tn(vc.astype(BF16), k_end[sl]), 0.0)
        s_scr[...] = s * e_last[c] + upd
    o = outs[0] if nchunk == 1 else jnp.concatenate(outs, axis=0)

    o = o * _segment_rms_scale(o, seg_ref, GLA_DV) * ng_ref[...]
    g = g_ref[0]
    o_ref[0] = o * (g * _sigmoid(g))
    sfin_ref[0] = s_scr[...]


def _gla_masks(tl, chunk):
    t = np.arange(min(tl, GLA_CUMSUM_SPAN))
    tril = ((t[:, None] // chunk == t[None, :] // chunk) & (t[None, :] <= t[:, None])).astype(np.float32)
    r = np.arange(GLA_HEADS * chunk)
    mk = (r[:, None] // chunk == np.arange(GLA_KW)[None, :] // GLA_DK).astype(np.float32)
    mv = (r[:, None] // chunk == np.arange(GLA_WIDTH)[None, :] // GLA_DV).astype(np.float32)
    ms = (np.arange(GLA_WIDTH)[:, None] // GLA_DV == np.arange(GLA_KW)[None, :] // GLA_DK).astype(np.float32)
    causal = (np.arange(chunk)[:, None] >= (r[None, :] % chunk)).astype(np.float32)
    seg = (np.arange(GLA_WIDTH)[:, None] // GLA_DV == np.arange(GLA_WIDTH)[None, :] // GLA_DV).astype(np.float32)
    return (jnp.asarray(tril, BF16), jnp.asarray(mk), jnp.asarray(mv), jnp.asarray(ms), jnp.asarray(causal),
            jnp.asarray(seg, BF16))


def _gla(qk, v, g, misc, a2p, ab, ng, s0t, tl, chunk):
    nb, seq, _ = qk.shape
    kern = functools.partial(_gla_kernel, tl=tl, chunk=chunk)
    masks = _gla_masks(tl, chunk)
    tok = lambda w: pl.BlockSpec((1, tl, w), lambda b, i: (b, i, 0))
    st = pl.BlockSpec((1, GLA_WIDTH, GLA_KW), lambda b, i: (b, 0, 0))
    return pl.pallas_call(
        kern,
        grid=(nb, seq // tl),
        in_specs=[tok(256), tok(GLA_WIDTH), tok(GLA_WIDTH), tok(LANES),
                  _const_spec((LANES, GLA_KW)), _const_spec((1, GLA_KW)), _const_spec((1, GLA_WIDTH)), st]
                 + [_const_spec(m.shape) for m in masks],
        out_specs=[tok(GLA_WIDTH), st],
        out_shape=[jax.ShapeDtypeStruct((nb, seq, GLA_WIDTH), F32),
                   jax.ShapeDtypeStruct((nb, GLA_WIDTH, GLA_KW), F32)],
        scratch_shapes=[pltpu.VMEM((GLA_WIDTH, GLA_KW), F32)],
        compiler_params=_cparams(("parallel", "arbitrary")),
        name="gla",
    )(qk, v, g, misc, a2p, ab, ng, s0t, *masks)


FOX_KAUG = 2 * FOX_HD


def _fox_constants():
    sel_k = np.zeros((FOX_HEADS // 2, 2 * LANES, 2 * FOX_KAUG), np.float32)
    sel_q = np.zeros((2 * FOX_KAUG, LANES), np.float32)
    ones_q = np.zeros((2 * FOX_KAUG, 1), np.float32)
    for hh in range(2):
        for d in range(FOX_HD):
            sel_k[:, hh * FOX_HD + d, hh * FOX_KAUG + d] = 1.0
            sel_q[hh * FOX_KAUG + d, hh * FOX_HD + d] = 1.0
        for part in range(3):
            ones_q[hh * FOX_KAUG + FOX_HD + part, 0] = 1.0
            for pair in range(FOX_HEADS // 2):
                sel_k[pair, LANES + part * FOX_HEADS + 2 * pair + hh, hh * FOX_KAUG + FOX_HD + part] = 1.0
    eye = np.eye(FOX_WIDTH, dtype=np.float32)
    t = np.arange(FOX_WIDTH)
    seg = (t[:, None] // FOX_HD == t[None, :] // FOX_HD).astype(np.float32)
    return (jnp.asarray(sel_k, BF16), jnp.asarray(sel_q, BF16), jnp.asarray(ones_q), jnp.asarray(eye, BF16),
            jnp.asarray(seg, BF16))


def _lower_ones(n):
    t = np.arange(n)
    return jnp.asarray((t[None, :] <= t[:, None]).astype(np.float32), BF16)


def _fox_kernel(*refs, tq, tqp, tk, seq, past, tkp):
    (fq_ref, fk_ref, fv_ref, lf_ref, qg_ref, kg_ref, seg_ref, selk_ref, selq_ref, onesq_ref, eye_ref, tril_ref) = refs[:12]
    refs = refs[12:]
    if past:
        pk_ref, pv_ref, plf_ref, trilp_ref = refs[:4]
        refs = refs[4:]
    refs = refs[1:]
    o_ref, krow_ref, kaug, vt, s_a, s_b, m_s, l_s, acc_s = refs[:9]
    if past:
        pkaug, pvt = refs[9:]
    i = pl.program_id(1)
    blk = min(tk, seq)

    def stage_block(k_bf, v_bf, logf, carry, tril, kaug_dst, vt_dst):
        n = logf.shape[0]
        logf128 = jnp.concatenate([logf, jnp.zeros((n, LANES - FOX_HEADS), F32)], axis=1)
        c = sum(_dot(tril, part) for part in _split3(logf128)) + carry
        hi, mid, lo = (part.astype(F32) for part in _split3(c * (-LOG2E)))
        packed = (hi + pltpu.roll(mid, FOX_HEADS, 1) + pltpu.roll(lo, 2 * FOX_HEADS, 1)).astype(BF16)
        for pair in range(FOX_HEADS // 2):
            x = jnp.concatenate([k_bf[:, pair * LANES:(pair + 1) * LANES], packed], axis=1)
            kaug_dst[0:n, pair * 2 * FOX_KAUG:(pair + 1) * 2 * FOX_KAUG] = _dot(x, selk_ref[pair]).astype(BF16)
        vt_dst[:, 0:n] = _dot_nt(eye_ref[...], v_bf).astype(BF16)
        return c[n - 1:n, :]

    @pl.when(i == 0)
    def _prologue():
        carry = jnp.zeros((1, LANES), F32)
        if past:
            for jb in range(past // tkp):
                r = slice(jb * tkp, (jb + 1) * tkp)
                carry = stage_block(pk_ref[0, r, :].astype(BF16), pv_ref[0, r, :].astype(BF16), plf_ref[0, r, :],
                                    carry, trilp_ref[...], pkaug.at[jb], pvt.at[jb])
        if blk < tk:
            kaug[...] = jnp.zeros_like(kaug)
            vt[...] = jnp.zeros_like(vt)
        for jb in range(max(seq // tk, 1)):
            r = slice(jb * tk, jb * tk + blk)
            kf = fk_ref[0, r, :]
            kn = kf * _segment_rms_scale(kf, seg_ref, FOX_HD) * kg_ref[...]
            krow_ref[0, r, :] = kn
            carry = stage_block(kn.astype(BF16), fv_ref[0, r, :].astype(BF16), lf_ref[0, r, :],
                                carry, tril_ref[...], kaug.at[jb], vt.at[jb])

    qf = fq_ref[0]
    qn = (qf * _segment_rms_scale(qf, seg_ref, FOX_HD) * qg_ref[...] * (FOX_HD ** -0.5 * LOG2E)).astype(BF16)
    if tqp > tq:
        qn = jnp.concatenate([qn, jnp.zeros((tqp - tq, FOX_WIDTH), BF16)], axis=0)
    qt_all = jnp.concatenate(
        [(_dot_nt(selq_ref[...], qn[:, pair * LANES:(pair + 1) * LANES]) + onesq_ref[...]).astype(BF16)
         for pair in range(FOX_HEADS // 2)], axis=0)
    key_i = lax.broadcasted_iota(jnp.int32, (tk, tqp), 0)
    qry_i = lax.broadcasted_iota(jnp.int32, (tk, tqp), 1)
    diag_ok = key_i <= qry_i

    klanes = [slice(h * FOX_KAUG, (h + 1) * FOX_KAUG) for h in range(FOX_HEADS)]
    vrows = [slice(h * FOX_HD, (h + 1) * FOX_HD) for h in range(FOX_HEADS)]
    qts = [qt_all[klanes[h], :] for h in range(FOX_HEADS)]

    def softmax_pv(h, s, v_t, mask):
        if mask is not None:
            s = jnp.where(mask, s, NEG_BIG)
        m = m_s[h]
        m_new = jnp.maximum(m, jnp.max(s, axis=0, keepdims=True))
        p = jnp.exp2(s - m_new)
        alpha = jnp.exp2(m - m_new)
        m_s[h] = m_new
        l_s[h] = alpha * l_s[h] + jnp.sum(p, axis=0, keepdims=True)
        acc_s[h] = alpha * acc_s[h] + _dot(v_t, p.astype(BF16))

    def block_step(j, src, dst):
        for h in range(FOX_HEADS):
            dst[h, 0:tk, :] = _dot(kaug[j + 1, :, klanes[h]], qts[h])
            softmax_pv(h, src[h, 0:tk, :], vt[j, vrows[h], :], None)

    for h in range(FOX_HEADS):
        m_s[h] = jnp.full((1, tqp), NEG_BIG, F32)
        l_s[h] = jnp.zeros((1, tqp), F32)
        acc_s[h] = jnp.zeros((FOX_HD, tqp), F32)
    if past:
        npb = past // tkp
        for h in range(FOX_HEADS):
            s_a[h, 0:tkp, :] = _dot(pkaug[0, :, klanes[h]], qts[h])
        src, dst = s_a, s_b
        for jb in range(npb):
            for h in range(FOX_HEADS):
                if jb + 1 < npb:
                    dst[h, 0:tkp, :] = _dot(pkaug[jb + 1, :, klanes[h]], qts[h])
                else:
                    dst[h, 0:tk, :] = _dot(kaug[0, :, klanes[h]], qts[h])
                softmax_pv(h, src[h, 0:tkp, :], pvt[jb, vrows[h], :], None)
            src, dst = dst, src
        if src is s_b:
            s_a[:, 0:tk, :] = s_b[:, 0:tk, :]
    else:
        for h in range(FOX_HEADS):
            s_a[h, 0:tk, :] = _dot(kaug[0, :, klanes[h]], qts[h])

    def body(jj, carry):
        block_step(2 * jj, s_a, s_b)
        block_step(2 * jj + 1, s_b, s_a)
        return carry

    lax.fori_loop(0, i // 2, body, 0)

    @pl.when(i % 2 == 1)
    def _odd_tail():
        block_step(i - 1, s_a, s_b)
        s_a[:, 0:tk, :] = s_b[:, 0:tk, :]

    outs = []
    for h in range(FOX_HEADS):
        softmax_pv(h, s_a[h, 0:tk, :], vt[i, vrows[h], :], diag_ok)
        outs.append(acc_s[h] / l_s[h])
    o_t = jnp.concatenate(outs, axis=0)
    o_ref[0] = o_t.T[0:tq, :]


def _fox(fq, fk, fv, logf, qg, kg, tq, tk, layer, krow_stack, past_k=None, past_v=None, past_logf=None):
    nb, seq, _ = fq.shape
    depth = fv.shape[0]
    nq = seq // tq
    tqp = max(tq, LANES)
    assert tqp == tk and (tq == tk or nq == 1)
    nkb = max(seq // tk, 1)
    past = 0 if past_k is None else past_k.shape[1]
    tkp = min(past, 256) if past else 0
    sel_k, sel_q, ones_q, eye, seg = _fox_constants()
    kern = functools.partial(_fox_kernel, tq=tq, tqp=tqp, tk=tk, seq=seq, past=past, tkp=tkp)
    full = lambda rows, w: pl.BlockSpec((1, rows, w), lambda b, i: (b, 0, 0))
    slab = lambda w: pl.BlockSpec((None, 1, seq, w), lambda b, i: (layer, b, 0, 0))
    consts = [qg, kg, seg, sel_k, sel_q, ones_q, eye, _lower_ones(min(tk, seq))]
    in_specs = [pl.BlockSpec((1, tq, FOX_WIDTH), lambda b, i: (b, i, 0)), full(seq, FOX_WIDTH), slab(FOX_WIDTH),
                slab(FOX_HEADS)] + [_const_spec(c.shape) for c in consts]
    args = [fq, fk, fv, logf] + consts
    scratch = [pltpu.VMEM((nkb, tk, FOX_HEADS * FOX_KAUG), BF16), pltpu.VMEM((nkb, FOX_WIDTH, tk), BF16),
               pltpu.VMEM((FOX_HEADS, max(tk, tkp), tqp), F32), pltpu.VMEM((FOX_HEADS, max(tk, tkp), tqp), F32),
               pltpu.VMEM((FOX_HEADS, 1, tqp), F32), pltpu.VMEM((FOX_HEADS, 1, tqp), F32),
               pltpu.VMEM((FOX_HEADS, FOX_HD, tqp), F32)]
    if past:
        in_specs += [full(past, FOX_WIDTH), full(past, FOX_WIDTH), full(past, FOX_HEADS), _const_spec((tkp, tkp))]
        args += [past_k, past_v, past_logf, _lower_ones(tkp)]
        scratch += [pltpu.VMEM((past // tkp, tkp, FOX_HEADS * FOX_KAUG), BF16),
                    pltpu.VMEM((past // tkp, FOX_WIDTH, tkp), BF16)]
    aliases = {len(args): 1}
    in_specs += [pl.BlockSpec(memory_space=pl.ANY)]
    args += [krow_stack]
    return pl.pallas_call(
        kern,
        grid=(nb, nq),
        in_specs=in_specs,
        out_specs=[pl.BlockSpec((1, tq, FOX_WIDTH), lambda b, i: (b, i, 0)), slab(FOX_WIDTH)],
        out_shape=[jax.ShapeDtypeStruct((nb, seq, FOX_WIDTH), F32),
                   jax.ShapeDtypeStruct((depth, nb, seq, FOX_WIDTH), F32)],
        input_output_aliases=aliases,
        scratch_shapes=scratch,
        compiler_params=_cparams(("parallel", "arbitrary")),
        name="fox",
    )(*args)


def _out_mlp_kernel(x_ref, s5_ref, gla_ref, fox_ref, wo_ref, g2_ref, wu_ref, wd_ref, y_ref, up_s):
    x1 = (x_ref[...]
          + _dot(s5_ref[...].astype(BF16), wo_ref[0:S5_WIDTH, :])
          + _dot(gla_ref[...].astype(BF16), wo_ref[S5_WIDTH:S5_WIDTH + GLA_WIDTH, :])
          + _dot(fox_ref[...].astype(BF16), wo_ref[S5_WIDTH + GLA_WIDTH:, :]))
    ms = jnp.mean(x1 * x1, axis=-1, keepdims=True)
    h2 = (x1 * lax.rsqrt(ms + EPS) * g2_ref[...]).astype(BF16)
    for c in range(D_FF // FF_CHUNK):
        cols = slice(c * FF_CHUNK, (c + 1) * FF_CHUNK)
        up = jnp.maximum(_dot(h2, wu_ref[:, cols]), 0.0)
        up_s[:, cols] = (up * up).astype(BF16)
    y_ref[...] = x1 + _dot(up_s[...], wd_ref[...])


def _out_mlp(x2d, s5, gla, fox, wo, g2, wu, wd, tm):
    n = x2d.shape[0]
    row = lambda w: pl.BlockSpec((tm, w), lambda i: (i, 0))
    resident = pl.BlockSpec(memory_space=pltpu.VMEM)
    return pl.pallas_call(
        _out_mlp_kernel,
        grid=(n // tm,),
        in_specs=[row(D_MODEL), row(S5_WIDTH), row(GLA_WIDTH), row(FOX_WIDTH),
                  resident, _const_spec((1, D_MODEL)), resident, resident],
        out_specs=row(D_MODEL),
        out_shape=jax.ShapeDtypeStruct((n, D_MODEL), F32),
        scratch_shapes=[pltpu.VMEM((tm, D_FF), BF16)],
        compiler_params=_cparams(("parallel",)),
        name="out_mlp",
    )(x2d, s5, gla, fox, wo, g2, wu, wd)


def _pack_params(norm1_g, w_in, s5_a_re, s5_a_im, s5_log_dt, s5_b_re, s5_b_im, s5_c_re, s5_c_im, s5_d,
                 s5_glu_w, s5_glu_b, gla_a2, gla_a_bias, gla_norm_g, fox_q_norm_g, fox_k_norm_g, fox_f_bias,
                 w_out, norm2_g, w_up, w_down):
    depth = w_in.shape[0]
    o_ga = 1024
    o_fq = o_ga + GLA_RANK
    o_ff = o_fq + 3 * FOX_WIDTH
    w_ff = w_in[:, :, o_ff:o_ff + FOX_HEADS]
    pad = jnp.zeros((depth, D_MODEL, LANES - FOX_HEADS - GLA_RANK), F32)
    w_pack = jnp.concatenate([w_in[:, :, :o_ga], w_ff, w_in[:, :, o_ga:o_fq], pad, w_in[:, :, o_fq:o_ff]],
                             axis=-1).astype(BF16)
    a2p = jnp.zeros((depth, LANES, GLA_KW), F32).at[:, MISC_GA:MISC_GA + GLA_RANK, :].set(gla_a2).astype(BF16)

    dt = jnp.exp(s5_log_dt)[..., None]
    mag = jnp.exp(dt * s5_a_re)
    abar_re = mag * jnp.cos(dt * s5_a_im)
    abar_im = mag * jnp.sin(dt * s5_a_im)
    den = s5_a_re * s5_a_re + s5_a_im * s5_a_im
    zr = ((abar_re - 1.0) * s5_a_re + abar_im * s5_a_im) / den
    zi = (abar_im * s5_a_re - (abar_re - 1.0) * s5_a_im) / den
    bb_re = zr[..., None] * s5_b_re - zi[..., None] * s5_b_im
    bb_im = zr[..., None] * s5_b_im + zi[..., None] * s5_b_re
    eye = jnp.eye(S5_GROUPS, dtype=F32)
    to_b = lambda bb: jnp.einsum('lgnp,gh->lgphn', bb, eye).reshape(depth, S5_WIDTH, S5_GN)
    bmat = jnp.concatenate([to_b(bb_re), to_b(bb_im)], axis=-1).astype(BF16)
    to_c = lambda cc: jnp.einsum('lgpn,gh->lgnhp', cc, eye).reshape(depth, S5_GN, S5_WIDTH)
    cmat = jnp.concatenate([to_c(s5_c_re), -to_c(s5_c_im)], axis=1).astype(BF16)

    return dict(
        norm1_g=norm1_g[:, None, :], w_pack=w_pack, fb_row=fox_f_bias[:, None, :],
        abar_re=abar_re.reshape(depth, 1, S5_GN), abar_im=abar_im.reshape(depth, 1, S5_GN),
        bmat=bmat, cmat=cmat, s5_d=s5_d.reshape(depth, 1, S5_WIDTH),
        glu_w=s5_glu_w.astype(BF16), glu_b=s5_glu_b[:, None, :],
        a2p=a2p, ab=gla_a_bias[:, None, :], ng=jnp.tile(gla_norm_g, (1, GLA_HEADS))[:, None, :],
        qg=jnp.tile(fox_q_norm_g, (1, FOX_HEADS))[:, None, :], kg=jnp.tile(fox_k_norm_g, (1, FOX_HEADS))[:, None, :],
        w_out=w_out.astype(BF16), norm2_g=norm2_g[:, None, :], w_up=w_up.astype(BF16), w_down=w_down.astype(BF16),
    )


def _gla_state_to_kernel(s):
    eye = jnp.eye(GLA_HEADS, dtype=s.dtype)
    return jnp.einsum('bhkv,hg->bhvgk', s, eye).reshape(s.shape[0], GLA_WIDTH, GLA_KW)


def _gla_state_from_kernel(st):
    nb = st.shape[0]
    s5d = st.reshape(nb, GLA_HEADS, GLA_DV, GLA_HEADS, GLA_DK)
    diag = jnp.stack([s5d[:, h, :, h, :] for h in range(GLA_HEADS)], axis=1)
    return jnp.swapaxes(diag, 2, 3)


def _tiles(nb, seq):
    n = nb * seq
    fox_tq = min(256, seq)
    return dict(in_rows=min(1024, n), out_rows=min(512, n), s5_tl=min(64, seq), gla_tl=min(1024, seq),
                gla_chunk=min(GLA_CHUNK, seq), fox_tq=fox_tq, fox_tk=max(fox_tq, LANES))


def _layer(x, p, l, depth, stacks, s5_h0r, s5_h0i, gla_s0, past_k, past_v, past_logf):
    nb, seq, _ = x.shape
    n = nb * seq
    t = _tiles(nb, seq)
    s5_tl, gla_tl, gla_chunk, fox_tq, fox_tk = t['s5_tl'], t['gla_tl'], t['gla_chunk'], t['fox_tq'], t['fox_tk']
    x2d = x.reshape(n, D_MODEL)
    u, qk, gv, gg, misc, fq, fk, fv_stack, logf_stack = _in_proj(
        x2d, p['norm1_g'][l], p['w_pack'][l], p['fb_row'][l], t['in_rows'], l, depth,
        stacks[:2])

    u_t = jnp.swapaxes(u.reshape(nb, seq, S5_WIDTH), 0, 1)
    o_s5_t, hr, hi = _s5(u_t, s5_h0r, s5_h0i, p['abar_re'][l], p['abar_im'][l], p['bmat'][l], p['cmat'][l],
                         p['s5_d'][l], p['glu_w'][l], p['glu_b'][l], s5_tl)
    o_s5 = jnp.swapaxes(o_s5_t, 0, 1).reshape(n, S5_WIDTH)

    b3 = lambda a: a.reshape(nb, seq, a.shape[-1])
    o_gla, s_t = _gla(b3(qk), b3(gv), b3(gg), b3(misc), p['a2p'][l], p['ab'][l], p['ng'][l],
                      _gla_state_to_kernel(gla_s0), gla_tl, gla_chunk)

    b4 = lambda a: a.reshape(depth, nb, seq, a.shape[-1])
    past = ()
    if past_k is not None:
        plen = past_k.shape[1]
        past = (past_k.reshape(nb, plen, FOX_WIDTH), past_v.reshape(nb, plen, FOX_WIDTH), past_logf)
    o_fox, krow_stack = _fox(b3(fq), b3(fk), b4(fv_stack), b4(logf_stack), p['qg'][l], p['kg'][l], fox_tq, fox_tk,
                             l, b4(stacks[2]), *past)

    y = _out_mlp(x2d, o_s5, o_gla.reshape(n, GLA_WIDTH), o_fox.reshape(n, FOX_WIDTH),
                 p['w_out'][l], p['norm2_g'][l], p['w_up'][l], p['w_down'][l], t['out_rows'])
    states = (hr.reshape(nb, S5_GROUPS, S5_STATE), hi.reshape(nb, S5_GROUPS, S5_STATE), _gla_state_from_kernel(s_t))
    return y.reshape(nb, seq, D_MODEL), states, (fv_stack, logf_stack, krow_stack)


def kernel(x_prompt, x_sample, state_s5_re, state_s5_im, state_gla, cache_fox_k, cache_fox_v, cache_fox_logf, norm1_g, w_in, s5_a_re, s5_a_im, s5_log_dt, s5_b_re, s5_b_im, s5_c_re, s5_c_im, s5_d, s5_glu_w, s5_glu_b, gla_a2, gla_a_bias, gla_norm_g, fox_q_norm_g, fox_k_norm_g, fox_f_bias, w_out, norm2_g, w_up, w_down):
    depth = w_in.shape[0]
    p = _pack_params(norm1_g, w_in, s5_a_re, s5_a_im, s5_log_dt, s5_b_re, s5_b_im, s5_c_re, s5_c_im, s5_d,
                     s5_glu_w, s5_glu_b, gla_a2, gla_a_bias, gla_norm_g, fox_q_norm_g, fox_k_norm_g, fox_f_bias,
                     w_out, norm2_g, w_up, w_down)
    bp, lp, _ = x_prompt.shape
    bs, ls, _ = x_sample.shape

    yp, ys = x_prompt, x_sample
    p_st = [[] for _ in range(3)]
    s_st = [[] for _ in range(3)]
    new_stacks = lambda n: (jnp.zeros((depth, n, FOX_WIDTH), F32), jnp.zeros((depth, n, FOX_HEADS), F32),
                            jnp.zeros((depth, n, FOX_WIDTH), F32))
    p_stacks, s_stacks = new_stacks(bp * lp), new_stacks(bs * ls)
    z_s5 = jnp.zeros((bp, S5_GN), F32)
    z_gla = jnp.zeros((bp, GLA_HEADS, GLA_DK, GLA_DV), F32)
    for l in range(depth):
        yp, stp, p_stacks = _layer(yp, p, l, depth, p_stacks, z_s5, z_s5, z_gla, None, None, None)
        ys, sts, s_stacks = _layer(ys, p, l, depth, s_stacks,
                                   state_s5_re[l].reshape(bs, S5_GN), state_s5_im[l].reshape(bs, S5_GN),
                                   state_gla[l], cache_fox_k[l], cache_fox_v[l], cache_fox_logf[l])
        for j in range(3):
            p_st[j].append(stp[j])
            s_st[j].append(sts[j])

    def fox_states(stacks, nb, seq):
        fv, logf, krow = stacks
        return (krow.reshape(depth, nb, seq, FOX_HEADS, FOX_HD), fv.reshape(depth, nb, seq, FOX_HEADS, FOX_HD),
                logf.reshape(depth, nb, seq, FOX_HEADS))

    return (yp, ys, *[jnp.stack(a) for a in p_st], *fox_states(p_stacks, bp, lp),
            *[jnp.stack(a) for a in s_st], *fox_states(s_stacks, bs, ls))
```

```python
import functools
import math

import jax
import jax.numpy as jnp
import numpy as np
from jax import lax
from jax.experimental import pallas as pl
from jax.experimental.pallas import tpu as pltpu

F32 = jnp.float32
BF16 = jnp.bfloat16

D_MODEL = 1024
EPS = 1e-6
LANES = 128

S5_WIDTH = 256
S5_GROUPS = 16
S5_GROUP = 16
S5_STATE = 64
S5_GN = S5_GROUPS * S5_STATE
S5_SUB_STEPS = 16

GLA_WIDTH = 256
GLA_HEADS = 4
GLA_DK = 32
GLA_DV = 64
GLA_KW = 128
GLA_RANK = 16
GLA_TAU = 16.0
GLA_CHUNK = 64
GLA_CUMSUM_SPAN = 256

FOX_WIDTH = 512
FOX_HD = 64
FOX_HEADS = 8
NEG_BIG = -1e30
LOG2E = math.log2(math.e)

D_FF = 4096
FF_CHUNK = 1024

COL_U = 0
COL_QK = 256
COL_GV = 512
COL_GG = 768
COL_MISC = 1024
COL_FQ = 1152
COL_FK = 1664
COL_FV = 2176
N_PACK = 2688
MISC_FF = 0
MISC_GA = 8

VMEM_LIMIT = 56 * 1024 * 1024


def _cparams(sem):
    return pltpu.CompilerParams(dimension_semantics=sem, vmem_limit_bytes=VMEM_LIMIT)


def _const_spec(shape):
    nd = len(shape)
    return pl.BlockSpec(shape, lambda *_: (0,) * nd)


def _log_sigmoid(x):
    return jnp.minimum(x, 0.0) - jnp.log1p(jnp.exp(-jnp.abs(x)))


def _sigmoid(x):
    return 1.0 / (1.0 + jnp.exp(-x))


def _split2(x):
    hi = x.astype(BF16)
    lo = (x - hi.astype(F32)).astype(BF16)
    return hi, lo


def _split3(x):
    hi = x.astype(BF16)
    r = x - hi.astype(F32)
    mid = r.astype(BF16)
    lo = (r - mid.astype(F32)).astype(BF16)
    return hi, mid, lo


def _dot(a, b):
    return jnp.dot(a, b, preferred_element_type=F32)


def _dot_nt(a, b):
    return lax.dot_general(a, b, (((1,), (1,)), ((), ())), preferred_element_type=F32)


def _dot_tn(a, b):
    return lax.dot_general(a, b, (((0,), (0,)), ((), ())), preferred_element_type=F32)


def _segment_rms_scale(x, seg_ref, width):
    hi, lo = _split2(x * x)
    seg = seg_ref[...]
    ms = (_dot(hi, seg) + _dot(lo, seg)) * (1.0 / width)
    return lax.rsqrt(ms + EPS)


def _in_proj_kernel(x_ref, g_ref, w_ref, fb_row_ref, *refs):
    u_ref, qk_ref, gv_ref, gg_ref, misc_ref, fq_ref, fk_ref, fv_ref, logf_ref = refs[2:]
    x = x_ref[...]
    ms = jnp.mean(x * x, axis=-1, keepdims=True)
    h = (x * lax.rsqrt(ms + EPS) * g_ref[...]).astype(BF16)
    for ref, c0 in ((u_ref, COL_U), (qk_ref, COL_QK), (gv_ref, COL_GV), (gg_ref, COL_GG),
                    (fq_ref, COL_FQ), (fk_ref, COL_FK), (fv_ref, COL_FV)):
        ref[...] = _dot(h, w_ref[:, c0:c0 + ref.shape[1]])
    misc = _dot(h, w_ref[:, COL_MISC:COL_MISC + LANES])
    misc_ref[...] = misc
    logf_ref[...] = _log_sigmoid(misc[:, MISC_FF:MISC_FF + FOX_HEADS] + fb_row_ref[...])


def _in_proj(x2d, g, w_pack, fb_row, tm, layer, depth, stacks):
    n = x2d.shape[0]
    row = lambda w: pl.BlockSpec((tm, w), lambda i: (i, 0))
    slab = lambda w: pl.BlockSpec((None, tm, w), lambda i: (layer, i, 0))
    widths = (S5_WIDTH, 256, GLA_WIDTH, GLA_WIDTH, LANES, FOX_WIDTH, FOX_WIDTH)
    in_specs = [row(D_MODEL), _const_spec((1, D_MODEL)), _const_spec((D_MODEL, N_PACK)), _const_spec((1, FOX_HEADS))]
    args = [x2d, g, w_pack, fb_row]
    aliases = {len(args): len(widths), len(args) + 1: len(widths) + 1}
    in_specs += [pl.BlockSpec(memory_space=pl.ANY)] * 2
    args += list(stacks)
    return pl.pallas_call(
        _in_proj_kernel,
        grid=(n // tm,),
        in_specs=in_specs,
        out_specs=[row(w) for w in widths] + [slab(FOX_WIDTH), slab(FOX_HEADS)],
        out_shape=[jax.ShapeDtypeStruct((n, w), F32) for w in widths]
                  + [jax.ShapeDtypeStruct((depth, n, FOX_WIDTH), F32), jax.ShapeDtypeStruct((depth, n, FOX_HEADS), F32)],
        input_output_aliases=aliases,
        compiler_params=_cparams(("parallel",)),
        name="in_proj",
    )(*args)


def _gelu_tanh(x):
    c = math.sqrt(2.0 / math.pi)
    return x * (0.5 * (1.0 + jnp.tanh(c * (x + 0.044715 * (x * x * x)))))


def _s5_kernel(u_ref, h0r_ref, h0i_ref, ar_ref, ai_ref, bmat_ref, cmat_ref, d_ref, gw_ref, gb_ref,
               o_ref, hr_out_ref, hi_out_ref, hbuf, hr_s, hi_s, *, tl, nb):
    i = pl.program_id(0)

    @pl.when(i == 0)
    def _():
        hr_s[...] = h0r_ref[...]
        hi_s[...] = h0i_ref[...]

    rows = tl * nb
    u = u_ref[...].reshape(rows, S5_WIDTH)
    ub = u.astype(BF16)
    sub = min(tl, S5_SUB_STEPS)
    srows = sub * nb
    re = slice(0, S5_GN)
    im = slice(S5_GN, 2 * S5_GN)

    def input_matmul(s):
        hbuf[s * srows:(s + 1) * srows, :] = _dot(ub[s * srows:(s + 1) * srows], bmat_ref[...])

    input_matmul(0)
    hr, hi = hr_s[...], hi_s[...]
    ys = []
    for s in range(tl // sub):
        if (s + 1) * sub < tl:
            input_matmul(s + 1)
        for t in range(s * sub, (s + 1) * sub):
            r = slice(t * nb, (t + 1) * nb)
            ar, ai = ar_ref[...], ai_ref[...]
            nhr = ar * hr - ai * hi + hbuf[r, re]
            nhi = ar * hi + ai * hr + hbuf[r, im]
            hbuf[r, re] = nhr
            hbuf[r, im] = nhi
            hr, hi = nhr, nhi
        ys.append(_dot(hbuf[s * srows:(s + 1) * srows, :].astype(BF16), cmat_ref[...]))
    hr_s[...] = hr
    hi_s[...] = hi

    y = (ys[0] if len(ys) == 1 else jnp.concatenate(ys, axis=0)) + d_ref[...] * u
    y = _gelu_tanh(y)
    z = _dot(y.astype(BF16), gw_ref[...]) + gb_ref[...]
    o_ref[...] = (y * _sigmoid(z)).reshape(tl, nb, S5_WIDTH)
    hr_out_ref[...] = hr_s[...]
    hi_out_ref[...] = hi_s[...]


def _s5(u_t, h0r, h0i, ar, ai, bmat, cmat, d, gw, gb, tl):
    seq, nb, _ = u_t.shape
    kern = functools.partial(_s5_kernel, tl=tl, nb=nb)
    state = _const_spec((nb, S5_GN))
    return pl.pallas_call(
        kern,
        grid=(seq // tl,),
        in_specs=[pl.BlockSpec((tl, nb, S5_WIDTH), lambda i: (i, 0, 0)), state, state,
                  _const_spec((1, S5_GN)), _const_spec((1, S5_GN)),
                  _const_spec((S5_WIDTH, 2 * S5_GN)), _const_spec((2 * S5_GN, S5_WIDTH)),
                  _const_spec((1, S5_WIDTH)), _const_spec((S5_WIDTH, S5_WIDTH)), _const_spec((1, S5_WIDTH))],
        out_specs=[pl.BlockSpec((tl, nb, S5_WIDTH), lambda i: (i, 0, 0)), state, state],
        out_shape=[jax.ShapeDtypeStruct((seq, nb, S5_WIDTH), F32),
                   jax.ShapeDtypeStruct((nb, S5_GN), F32), jax.ShapeDtypeStruct((nb, S5_GN), F32)],
        scratch_shapes=[pltpu.VMEM((tl * nb, 2 * S5_GN), F32),
                        pltpu.VMEM((nb, S5_GN), F32), pltpu.VMEM((nb, S5_GN), F32)],
        compiler_params=_cparams(("arbitrary",)),
        name="s5",
    )(u_t, h0r, h0i, ar, ai, bmat, cmat, d, gw, gb)


def _gla_kernel(qk_ref, v_ref, g_ref, misc_ref, a2_ref, ab_ref, ng_ref, s0_ref,
                tril_ref, mk_ref, mv_ref, ms_ref, causal_ref, seg_ref,
                o_ref, sfin_ref, s_scr, *, tl, chunk):
    i = pl.program_id(1)

    @pl.when(i == 0)
    def _():
        s_scr[...] = s0_ref[0]

    qk = qk_ref[0]
    q = qk[:, :GLA_KW] * (GLA_DK ** -0.5)
    k = qk[:, GLA_KW:]
    v = v_ref[0]
    la = _log_sigmoid(_dot(misc_ref[0].astype(BF16), a2_ref[...]) + ab_ref[...]) * (1.0 / GLA_TAU)

    tril = tril_ref[...]
    span = tril.shape[0]
    parts = _split3(la)
    b = jnp.concatenate([sum(_dot(tril, part[r0:r0 + span]) for part in parts) for r0 in range(0, tl, span)], axis=0)
    nchunk = tl // chunk
    b3 = b.reshape(nchunk, chunk, GLA_KW)
    b_last = b3[:, chunk - 1:chunk, :]
    b_end = (b_last - b3).reshape(tl, GLA_KW)
    q_dec = (q * jnp.exp(b)).astype(BF16)
    k_inv = k * jnp.exp(-b)
    k_end = (k * jnp.exp(b_end)).astype(BF16)
    e_last = jnp.exp(b_last)

    outs = []
    for c in range(nchunk):
        sl = slice(c * chunk, (c + 1) * chunk)
        qd = q_dec[sl]
        kbd = jnp.where(mk_ref[...] > 0, jnp.concatenate([k_inv[sl]] * GLA_HEADS, axis=0), 0.0).astype(BF16)
        att = jnp.where(causal_ref[...] > 0, _dot_nt(qd, kbd), 0.0)
        vc = v[sl]
        vbd = jnp.where(mv_ref[...] > 0, jnp.concatenate([vc] * GLA_HEADS, axis=0), 0.0).astype(BF16)
        s = s_scr[...]
        outs.append(_dot(att.astype(BF16), vbd) + _dot_nt(qd, s.astype(BF16)))
        upd = jnp.where(ms_ref[...] > 0, _dot_tn(vc.astype(BF16), k_end[sl]), 0.0)
        s_scr[...] = s * e_last[c] + upd
    o = outs[0] if nchunk == 1 else jnp.concatenate(outs, axis=0)

    o = o * _segment_rms_scale(o, seg_ref, GLA_DV) * ng_ref[...]
    g = g_ref[0]
    o_ref[0] = o * (g * _sigmoid(g))
    sfin_ref[0] = s_scr[...]


def _gla_masks(tl, chunk):
    t = np.arange(min(tl, GLA_CUMSUM_SPAN))
    tril = ((t[:, None] // chunk == t[None, :] // chunk) & (t[None, :] <= t[:, None])).astype(np.float32)
    r = np.arange(GLA_HEADS * chunk)
    mk = (r[:, None] // chunk == np.arange(GLA_KW)[None, :] // GLA_DK).astype(np.float32)
    mv = (r[:, None] // chunk == np.arange(GLA_WIDTH)[None, :] // GLA_DV).astype(np.float32)
    ms = (np.arange(GLA_WIDTH)[:, None] // GLA_DV == np.arange(GLA_KW)[None, :] // GLA_DK).astype(np.float32)
    causal = (np.arange(chunk)[:, None] >= (r[None, :] % chunk)).astype(np.float32)
    seg = (np.arange(GLA_WIDTH)[:, None] // GLA_DV == np.arange(GLA_WIDTH)[None, :] // GLA_DV).astype(np.float32)
    return (jnp.asarray(tril, BF16), jnp.asarray(mk), jnp.asarray(mv), jnp.asarray(ms), jnp.asarray(causal),
            jnp.asarray(seg, BF16))


def _gla(qk, v, g, misc, a2p, ab, ng, s0t, tl, chunk):
    nb, seq, _ = qk.shape
    kern = functools.partial(_gla_kernel, tl=tl, chunk=chunk)
    masks = _gla_masks(tl, chunk)
    tok = lambda w: pl.BlockSpec((1, tl, w), lambda b, i: (b, i, 0))
    st = pl.BlockSpec((1, GLA_WIDTH, GLA_KW), lambda b, i: (b, 0, 0))
    return pl.pallas_call(
        kern,
        grid=(nb, seq // tl),
        in_specs=[tok(256), tok(GLA_WIDTH), tok(GLA_WIDTH), tok(LANES),
                  _const_spec((LANES, GLA_KW)), _const_spec((1, GLA_KW)), _const_spec((1, GLA_WIDTH)), st]
                 + [_const_spec(m.shape) for m in masks],
        out_specs=[tok(GLA_WIDTH), st],
        out_shape=[jax.ShapeDtypeStruct((nb, seq, GLA_WIDTH), F32),
                   jax.ShapeDtypeStruct((nb, GLA_WIDTH, GLA_KW), F32)],
        scratch_shapes=[pltpu.VMEM((GLA_WIDTH, GLA_KW), F32)],
        compiler_params=_cparams(("parallel", "arbitrary")),
        name="gla",
    )(qk, v, g, misc, a2p, ab, ng, s0t, *masks)


FOX_KAUG = 2 * FOX_HD


def _fox_constants():
    sel_k = np.zeros((FOX_HEADS // 2, 2 * LANES, 2 * FOX_KAUG), np.float32)
    sel_q = np.zeros((2 * FOX_KAUG, LANES), np.float32)
    ones_q = np.zeros((2 * FOX_KAUG, 1), np.float32)
    for hh in range(2):
        for d in range(FOX_HD):
            sel_k[:, hh * FOX_HD + d, hh * FOX_KAUG + d] = 1.0
            sel_q[hh * FOX_KAUG + d, hh * FOX_HD + d] = 1.0
        for part in range(3):
            ones_q[hh * FOX_KAUG + FOX_HD + part, 0] = 1.0
            for pair in range(FOX_HEADS // 2):
                sel_k[pair, LANES + part * FOX_HEADS + 2 * pair + hh, hh * FOX_KAUG + FOX_HD + part] = 1.0
    eye = np.eye(FOX_WIDTH, dtype=np.float32)
    t = np.arange(FOX_WIDTH)
    seg = (t[:, None] // FOX_HD == t[None, :] // FOX_HD).astype(np.float32)
    return (jnp.asarray(sel_k, BF16), jnp.asarray(sel_q, BF16), jnp.asarray(ones_q), jnp.asarray(eye, BF16),
            jnp.asarray(seg, BF16))


def _lower_ones(n):
    t = np.arange(n)
    return jnp.asarray((t[None, :] <= t[:, None]).astype(np.float32), BF16)


def _fox_kernel(*refs, tq, tqp, tk, seq, past, tkp):
    (fq_ref, fk_ref, fv_ref, lf_ref, qg_ref, kg_ref, seg_ref, selk_ref, selq_ref, onesq_ref, eye_ref, tril_ref) = refs[:12]
    refs = refs[12:]
    if past:
        pk_ref, pv_ref, plf_ref, trilp_ref = refs[:4]
        refs = refs[4:]
    refs = refs[1:]
    o_ref, krow_ref, kaug, vt, s_a, s_b, m_s, l_s, acc_s = refs[:9]
    if past:
        pkaug, pvt = refs[9:]
    i = pl.program_id(1)
    blk = min(tk, seq)

    def stage_block(k_bf, v_bf, logf, carry, tril, kaug_dst, vt_dst):
        n = logf.shape[0]
        logf128 = jnp.concatenate([logf, jnp.zeros((n, LANES - FOX_HEADS), F32)], axis=1)
        c = sum(_dot(tril, part) for part in _split3(logf128)) + carry
        hi, mid, lo = (part.astype(F32) for part in _split3(c * (-LOG2E)))
        packed = (hi + pltpu.roll(mid, FOX_HEADS, 1) + pltpu.roll(lo, 2 * FOX_HEADS, 1)).astype(BF16)
        for pair in range(FOX_HEADS // 2):
            x = jnp.concatenate([k_bf[:, pair * LANES:(pair + 1) * LANES], packed], axis=1)
            kaug_dst[0:n, pair * 2 * FOX_KAUG:(pair + 1) * 2 * FOX_KAUG] = _dot(x, selk_ref[pair]).astype(BF16)
        vt_dst[:, 0:n] = _dot_nt(eye_ref[...], v_bf).astype(BF16)
        return c[n - 1:n, :]

    @pl.when(i == 0)
    def _prologue():
        carry = jnp.zeros((1, LANES), F32)
        if past:
            for jb in range(past // tkp):
                r = slice(jb * tkp, (jb + 1) * tkp)
                carry = stage_block(pk_ref[0, r, :].astype(BF16), pv_ref[0, r, :].astype(BF16), plf_ref[0, r, :],
                                    carry, trilp_ref[...], pkaug.at[jb], pvt.at[jb])
        if blk < tk:
            kaug[...] = jnp.zeros_like(kaug)
            vt[...] = jnp.zeros_like(vt)
        for jb in range(max(seq // tk, 1)):
            r = slice(jb * tk, jb * tk + blk)
            kf = fk_ref[0, r, :]
            kn = kf * _segment_rms_scale(kf, seg_ref, FOX_HD) * kg_ref[...]
            krow_ref[0, r, :] = kn
            carry = stage_block(kn.astype(BF16), fv_ref[0, r, :].astype(BF16), lf_ref[0, r, :],
                                carry, tril_ref[...], kaug.at[jb], vt.at[jb])

    qf = fq_ref[0]
    qn = (qf * _segment_rms_scale(qf, seg_ref, FOX_HD) * qg_ref[...] * (FOX_HD ** -0.5 * LOG2E)).astype(BF16)
    if tqp > tq:
        qn = jnp.concatenate([qn, jnp.zeros((tqp - tq, FOX_WIDTH), BF16)], axis=0)
    qt_all = jnp.concatenate(
        [(_dot_nt(selq_ref[...], qn[:, pair * LANES:(pair + 1) * LANES]) + onesq_ref[...]).astype(BF16)
         for pair in range(FOX_HEADS // 2)], axis=0)
    key_i = lax.broadcasted_iota(jnp.int32, (tk, tqp), 0)
    qry_i = lax.broadcasted_iota(jnp.int32, (tk, tqp), 1)
    diag_ok = key_i <= qry_i

    klanes = [slice(h * FOX_KAUG, (h + 1) * FOX_KAUG) for h in range(FOX_HEADS)]
    vrows = [slice(h * FOX_HD, (h + 1) * FOX_HD) for h in range(FOX_HEADS)]
    qts = [qt_all[klanes[h], :] for h in range(FOX_HEADS)]

    def softmax_pv(h, s, v_t, mask):
        if mask is not None:
            s = jnp.where(mask, s, NEG_BIG)
        m = m_s[h]
        m_new = jnp.maximum(m, jnp.max(s, axis=0, keepdims=True))
        p = jnp.exp2(s - m_new)
        alpha = jnp.exp2(m - m_new)
        m_s[h] = m_new
        l_s[h] = alpha * l_s[h] + jnp.sum(p, axis=0, keepdims=True)
        acc_s[h] = alpha * acc_s[h] + _dot(v_t, p.astype(BF16))

    def block_step(j, src, dst):
        for h in range(FOX_HEADS):
            dst[h, 0:tk, :] = _dot(kaug[j + 1, :, klanes[h]], qts[h])
            softmax_pv(h, src[h, 0:tk, :], vt[j, vrows[h], :], None)

    for h in range(FOX_HEADS):
        m_s[h] = jnp.full((1, tqp), NEG_BIG, F32)
        l_s[h] = jnp.zeros((1, tqp), F32)
        acc_s[h] = jnp.zeros((FOX_HD, tqp), F32)
    if past:
        npb = past // tkp
        for h in range(FOX_HEADS):
            s_a[h, 0:tkp, :] = _dot(pkaug[0, :, klanes[h]], qts[h])
        src, dst = s_a, s_b
        for jb in range(npb):
            for h in range(FOX_HEADS):
                if jb + 1 < npb:
                    dst[h, 0:tkp, :] = _dot(pkaug[jb + 1, :, klanes[h]], qts[h])
                else:
                    dst[h, 0:tk, :] = _dot(kaug[0, :, klanes[h]], qts[h])
                softmax_pv(h, src[h, 0:tkp, :], pvt[jb, vrows[h], :], None)
            src, dst = dst, src
        if src is s_b:
            s_a[:, 0:tk, :] = s_b[:, 0:tk, :]
    else:
        for h in range(FOX_HEADS):
            s_a[h, 0:tk, :] = _dot(kaug[0, :, klanes[h]], qts[h])

    def body(jj, carry):
        block_step(2 * jj, s_a, s_b)
        block_step(2 * jj + 1, s_b, s_a)
        return carry

    lax.fori_loop(0, i // 2, body, 0)

    @pl.when(i % 2 == 1)
    def _odd_tail():
        block_step(i - 1, s_a, s_b)
        s_a[:, 0:tk, :] = s_b[:, 0:tk, :]

    outs = []
    for h in range(FOX_HEADS):
        softmax_pv(h, s_a[h, 0:tk, :], vt[i, vrows[h], :], diag_ok)
        outs.append(acc_s[h] / l_s[h])
    o_t = jnp.concatenate(outs, axis=0)
    o_ref[0] = o_t.T[0:tq, :]


def _fox(fq, fk, fv, logf, qg, kg, tq, tk, layer, krow_stack, past_k=None, past_v=None, past_logf=None):
    nb, seq, _ = fq.shape
    depth = fv.shape[0]
    nq = seq // tq
    tqp = max(tq, LANES)
    assert tqp == tk and (tq == tk or nq == 1)
    nkb = max(seq // tk, 1)
    past = 0 if past_k is None else past_k.shape[1]
    tkp = min(past, 256) if past else 0
    sel_k, sel_q, ones_q, eye, seg = _fox_constants()
    kern = functools.partial(_fox_kernel, tq=tq, tqp=tqp, tk=tk, seq=seq, past=past, tkp=tkp)
    full = lambda rows, w: pl.BlockSpec((1, rows, w), lambda b, i: (b, 0, 0))
    slab = lambda w: pl.BlockSpec((None, 1, seq, w), lambda b, i: (layer, b, 0, 0))
    consts = [qg, kg, seg, sel_k, sel_q, ones_q, eye, _lower_ones(min(tk, seq))]
    in_specs = [pl.BlockSpec((1, tq, FOX_WIDTH), lambda b, i: (b, i, 0)), full(seq, FOX_WIDTH), slab(FOX_WIDTH),
                slab(FOX_HEADS)] + [_const_spec(c.shape) for c in consts]
    args = [fq, fk, fv, logf] + consts
    scratch = [pltpu.VMEM((nkb, tk, FOX_HEADS * FOX_KAUG), BF16), pltpu.VMEM((nkb, FOX_WIDTH, tk), BF16),
               pltpu.VMEM((FOX_HEADS, max(tk, tkp), tqp), F32), pltpu.VMEM((FOX_HEADS, max(tk, tkp), tqp), F32),
               pltpu.VMEM((FOX_HEADS, 1, tqp), F32), pltpu.VMEM((FOX_HEADS, 1, tqp), F32),
               pltpu.VMEM((FOX_HEADS, FOX_HD, tqp), F32)]
    if past:
        in_specs += [full(past, FOX_WIDTH), full(past, FOX_WIDTH), full(past, FOX_HEADS), _const_spec((tkp, tkp))]
        args += [past_k, past_v, past_logf, _lower_ones(tkp)]
        scratch += [pltpu.VMEM((past // tkp, tkp, FOX_HEADS * FOX_KAUG), BF16),
                    pltpu.VMEM((past // tkp, FOX_WIDTH, tkp), BF16)]
    aliases = {len(args): 1}
    in_specs += [pl.BlockSpec(memory_space=pl.ANY)]
    args += [krow_stack]
    return pl.pallas_call(
        kern,
        grid=(nb, nq),
        in_specs=in_specs,
        out_specs=[pl.BlockSpec((1, tq, FOX_WIDTH), lambda b, i: (b, i, 0)), slab(FOX_WIDTH)],
        out_shape=[jax.ShapeDtypeStruct((nb, seq, FOX_WIDTH), F32),
                   jax.ShapeDtypeStruct((depth, nb, seq, FOX_WIDTH), F32)],
        input_output_aliases=aliases,
        scratch_shapes=scratch,
        compiler_params=_cparams(("parallel", "arbitrary")),
        name="fox",
    )(*args)


def _out_mlp_kernel(x_ref, s5_ref, gla_ref, fox_ref, wo_ref, g2_ref, wu_ref, wd_ref, y_ref, up_s):
    x1 = (x_ref[...]
          + _dot(s5_ref[...].astype(BF16), wo_ref[0:S5_WIDTH, :])
          + _dot(gla_ref[...].astype(BF16), wo_ref[S5_WIDTH:S5_WIDTH + GLA_WIDTH, :])
          + _dot(fox_ref[...].astype(BF16), wo_ref[S5_WIDTH + GLA_WIDTH:, :]))
    ms = jnp.mean(x1 * x1, axis=-1, keepdims=True)
    h2 = (x1 * lax.rsqrt(ms + EPS) * g2_ref[...]).astype(BF16)
    for c in range(D_FF // FF_CHUNK):
        cols = slice(c * FF_CHUNK, (c + 1) * FF_CHUNK)
        up = jnp.maximum(_dot(h2, wu_ref[:, cols]), 0.0)
        up_s[:, cols] = (up * up).astype(BF16)
    y_ref[...] = x1 + _dot(up_s[...], wd_ref[...])


def _out_mlp(x2d, s5, gla, fox, wo, g2, wu, wd, tm):
    n = x2d.shape[0]
    row = lambda w: pl.BlockSpec((tm, w), lambda i: (i, 0))
    resident = pl.BlockSpec(memory_space=pltpu.VMEM)
    return pl.pallas_call(
        _out_mlp_kernel,
        grid=(n // tm,),
        in_specs=[row(D_MODEL), row(S5_WIDTH), row(GLA_WIDTH), row(FOX_WIDTH),
                  resident, _const_spec((1, D_MODEL)), resident, resident],
        out_specs=row(D_MODEL),
        out_shape=jax.ShapeDtypeStruct((n, D_MODEL), F32),
        scratch_shapes=[pltpu.VMEM((tm, D_FF), BF16)],
        compiler_params=_cparams(("parallel",)),
        name="out_mlp",
    )(x2d, s5, gla, fox, wo, g2, wu, wd)


def _pack_params(norm1_g, w_in, s5_a_re, s5_a_im, s5_log_dt, s5_b_re, s5_b_im, s5_c_re, s5_c_im, s5_d,
                 s5_glu_w, s5_glu_b, gla_a2, gla_a_bias, gla_norm_g, fox_q_norm_g, fox_k_norm_g, fox_f_bias,
                 w_out, norm2_g, w_up, w_down):
    depth = w_in.shape[0]
    o_ga = 1024
    o_fq = o_ga + GLA_RANK
    o_ff = o_fq + 3 * FOX_WIDTH
    w_ff = w_in[:, :, o_ff:o_ff + FOX_HEADS]
    pad = jnp.zeros((depth, D_MODEL, LANES - FOX_HEADS - GLA_RANK), F32)
    w_pack = jnp.concatenate([w_in[:, :, :o_ga], w_ff, w_in[:, :, o_ga:o_fq], pad, w_in[:, :, o_fq:o_ff]],
                             axis=-1).astype(BF16)
    a2p = jnp.zeros((depth, LANES, GLA_KW), F32).at[:, MISC_GA:MISC_GA + GLA_RANK, :].set(gla_a2).astype(BF16)

    dt = jnp.exp(s5_log_dt)[..., None]
    mag = jnp.exp(dt * s5_a_re)
    abar_re = mag * jnp.cos(dt * s5_a_im)
    abar_im = mag * jnp.sin(dt * s5_a_im)
    den = s5_a_re * s5_a_re + s5_a_im * s5_a_im
    zr = ((abar_re - 1.0) * s5_a_re + abar_im * s5_a_im) / den
    zi = (abar_im * s5_a_re - (abar_re - 1.0) * s5_a_im) / den
    bb_re = zr[..., None] * s5_b_re - zi[..., None] * s5_b_im
    bb_im = zr[..., None] * s5_b_im + zi[..., None] * s5_b_re
    eye = jnp.eye(S5_GROUPS, dtype=F32)
    to_b = lambda bb: jnp.einsum('lgnp,gh->lgphn', bb, eye).reshape(depth, S5_WIDTH, S5_GN)
    bmat = jnp.concatenate([to_b(bb_re), to_b(bb_im)], axis=-1).astype(BF16)
    to_c = lambda cc: jnp.einsum('lgpn,gh->lgnhp', cc, eye).reshape(depth, S5_GN, S5_WIDTH)
    cmat = jnp.concatenate([to_c(s5_c_re), -to_c(s5_c_im)], axis=1).astype(BF16)

    return dict(
        norm1_g=norm1_g[:, None, :], w_pack=w_pack, fb_row=fox_f_bias[:, None, :],
        abar_re=abar_re.reshape(depth, 1, S5_GN), abar_im=abar_im.reshape(depth, 1, S5_GN),
        bmat=bmat, cmat=cmat, s5_d=s5_d.reshape(depth, 1, S5_WIDTH),
        glu_w=s5_glu_w.astype(BF16), glu_b=s5_glu_b[:, None, :],
        a2p=a2p, ab=gla_a_bias[:, None, :], ng=jnp.tile(gla_norm_g, (1, GLA_HEADS))[:, None, :],
        qg=jnp.tile(fox_q_norm_g, (1, FOX_HEADS))[:, None, :], kg=jnp.tile(fox_k_norm_g, (1, FOX_HEADS))[:, None, :],
        w_out=w_out.astype(BF16), norm2_g=norm2_g[:, None, :], w_up=w_up.astype(BF16), w_down=w_down.astype(BF16),
    )


def _gla_state_to_kernel(s):
    eye = jnp.eye(GLA_HEADS, dtype=s.dtype)
    return jnp.einsum('bhkv,hg->bhvgk', s, eye).reshape(s.shape[0], GLA_WIDTH, GLA_KW)


def _gla_state_from_kernel(st):
    nb = st.shape[0]
    s5d = st.reshape(nb, GLA_HEADS, GLA_DV, GLA_HEADS, GLA_DK)
    diag = jnp.stack([s5d[:, h, :, h, :] for h in range(GLA_HEADS)], axis=1)
    return jnp.swapaxes(diag, 2, 3)


def _tiles(nb, seq):
    n = nb * seq
    fox_tq = min(256, seq)
    return dict(in_rows=min(1024, n), out_rows=min(512, n), s5_tl=min(64, seq), gla_tl=min(1024, seq),
                gla_chunk=min(GLA_CHUNK, seq), fox_tq=fox_tq, fox_tk=max(fox_tq, LANES))


def _layer(x, p, l, depth, stacks, s5_h0r, s5_h0i, gla_s0, past_k, past_v, past_logf):
    nb, seq, _ = x.shape
    n = nb * seq
    t = _tiles(nb, seq)
    s5_tl, gla_tl, gla_chunk, fox_tq, fox_tk = t['s5_tl'], t['gla_tl'], t['gla_chunk'], t['fox_tq'], t['fox_tk']
    x2d = x.reshape(n, D_MODEL)
    u, qk, gv, gg, misc, fq, fk, fv_stack, logf_stack = _in_proj(
        x2d, p['norm1_g'][l], p['w_pack'][l], p['fb_row'][l], t['in_rows'], l, depth,
        stacks[:2])

    u_t = jnp.swapaxes(u.reshape(nb, seq, S5_WIDTH), 0, 1)
    o_s5_t, hr, hi = _s5(u_t, s5_h0r, s5_h0i, p['abar_re'][l], p['abar_im'][l], p['bmat'][l], p['cmat'][l],
                         p['s5_d'][l], p['glu_w'][l], p['glu_b'][l], s5_tl)
    o_s5 = jnp.swapaxes(o_s5_t, 0, 1).reshape(n, S5_WIDTH)

    b3 = lambda a: a.reshape(nb, seq, a.shape[-1])
    o_gla, s_t = _gla(b3(qk), b3(gv), b3(gg), b3(misc), p['a2p'][l], p['ab'][l], p['ng'][l],
                      _gla_state_to_kernel(gla_s0), gla_tl, gla_chunk)

    b4 = lambda a: a.reshape(depth, nb, seq, a.shape[-1])
    past = ()
    if past_k is not None:
        plen = past_k.shape[1]
        past = (past_k.reshape(nb, plen, FOX_WIDTH), past_v.reshape(nb, plen, FOX_WIDTH), past_logf)
    o_fox, krow_stack = _fox(b3(fq), b3(fk), b4(fv_stack), b4(logf_stack), p['qg'][l], p['kg'][l], fox_tq, fox_tk,
                             l, b4(stacks[2]), *past)

    y = _out_mlp(x2d, o_s5, o_gla.reshape(n, GLA_WIDTH), o_fox.reshape(n, FOX_WIDTH),
                 p['w_out'][l], p['norm2_g'][l], p['w_up'][l], p['w_down'][l], t['out_rows'])
    states = (hr.reshape(nb, S5_GROUPS, S5_STATE), hi.reshape(nb, S5_GROUPS, S5_STATE), _gla_state_from_kernel(s_t))
    return y.reshape(nb, seq, D_MODEL), states, (fv_stack, logf_stack, krow_stack)


def kernel(x_prompt, x_sample, state_s5_re, state_s5_im, state_gla, cache_fox_k, cache_fox_v, cache_fox_logf, norm1_g, w_in, s5_a_re, s5_a_im, s5_log_dt, s5_b_re, s5_b_im, s5_c_re, s5_c_im, s5_d, s5_glu_w, s5_glu_b, gla_a2, gla_a_bias, gla_norm_g, fox_q_norm_g, fox_k_norm_g, fox_f_bias, w_out, norm2_g, w_up, w_down):
    depth = w_in.shape[0]
    p = _pack_params(norm1_g, w_in, s5_a_re, s5_a_im, s5_log_dt, s5_b_re, s5_b_im, s5_c_re, s5_c_im, s5_d,
                     s5_glu_w, s5_glu_b, gla_a2, gla_a_bias, gla_norm_g, fox_q_norm_g, fox_k_norm_g, fox_f_bias,
                     w_out, norm2_g, w_up, w_down)
    bp, lp, _ = x_prompt.shape
    bs, ls, _ = x_sample.shape

    yp, ys = x_prompt, x_sample
    p_st = [[] for _ in range(3)]
    s_st = [[] for _ in range(3)]
    new_stacks = lambda n: (jnp.zeros((depth, n, FOX_WIDTH), F32), jnp.zeros((depth, n, FOX_HEADS), F32),
                            jnp.zeros((depth, n, FOX_WIDTH), F32))
    p_stacks, s_stacks = new_stacks(bp * lp), new_stacks(bs * ls)
    z_s5 = jnp.zeros((bp, S5_GN), F32)
    z_gla = jnp.zeros((bp, GLA_HEADS, GLA_DK, GLA_DV), F32)
    for l in range(depth):
        yp, stp, p_stacks = _layer(yp, p, l, depth, p_stacks, z_s5, z_s5, z_gla, None, None, None)
        ys, sts, s_stacks = _layer(ys, p, l, depth, s_stacks,
                                   state_s5_re[l].reshape(bs, S5_GN), state_s5_im[l].reshape(bs, S5_GN),
                                   state_gla[l], cache_fox_k[l], cache_fox_v[l], cache_fox_logf[l])
        for j in range(3):
            p_st[j].append(stp[j])
            s_st[j].append(sts[j])

    def fox_states(stacks, nb, seq):
        fv, logf, krow = stacks
        return (krow.reshape(depth, nb, seq, FOX_HEADS, FOX_HD), fv.reshape(depth, nb, seq, FOX_HEADS, FOX_HD),
                logf.reshape(depth, nb, seq, FOX_HEADS))

    return (yp, ys, *[jnp.stack(a) for a in p_st], *fox_states(p_stacks, bp, lp),
            *[jnp.stack(a) for a in s_st], *fox_states(s_stacks, bs, ls))
```

```python
import functools
import math

import jax
import jax.numpy as jnp
import numpy as np
from jax import lax
from jax.experimental import pallas as pl
from jax.experimental.pallas import tpu as pltpu

F32 = jnp.float32
BF16 = jnp.bfloat16

D_MODEL = 1024
EPS = 1e-6
LANES = 128

S5_WIDTH = 256
S5_GROUPS = 16
S5_GROUP = 16
S5_STATE = 64
S5_GN = S5_GROUPS * S5_STATE
S5_SUB_STEPS = 16

GLA_WIDTH = 256
GLA_HEADS = 4
GLA_DK = 32
GLA_DV = 64
GLA_KW = 128
GLA_RANK = 16
GLA_TAU = 16.0
GLA_CHUNK = 64
GLA_CUMSUM_SPAN = 256

FOX_WIDTH = 512
FOX_HD = 64
FOX_HEADS = 8
NEG_BIG = -1e30
LOG2E = math.log2(math.e)

D_FF = 4096
FF_CHUNK = 1024

COL_U = 0
COL_QK = 256
COL_GV = 512
COL_GG = 768
COL_MISC = 1024
COL_FQ = 1152
COL_FK = 1664
COL_FV = 2176
N_PACK = 2688
MISC_FF = 0
MISC_GA = 8

VMEM_LIMIT = 56 * 1024 * 1024


def _cparams(sem):
    return pltpu.CompilerParams(dimension_semantics=sem, vmem_limit_bytes=VMEM_LIMIT)


def _const_spec(shape):
    nd = len(shape)
    return pl.BlockSpec(shape, lambda *_: (0,) * nd)


def _log_sigmoid(x):
    return jnp.minimum(x, 0.0) - jnp.log1p(jnp.exp(-jnp.abs(x)))


def _sigmoid(x):
    return 1.0 / (1.0 + jnp.exp(-x))


def _split2(x):
    hi = x.astype(BF16)
    lo = (x - hi.astype(F32)).astype(BF16)
    return hi, lo


def _split3(x):
    hi = x.astype(BF16)
    r = x - hi.astype(F32)
    mid = r.astype(BF16)
    lo = (r - mid.astype(F32)).astype(BF16)
    return hi, mid, lo


def _dot(a, b):
    return jnp.dot(a, b, preferred_element_type=F32)


def _dot_nt(a, b):
    return lax.dot_general(a, b, (((1,), (1,)), ((), ())), preferred_element_type=F32)


def _dot_tn(a, b):
    return lax.dot_general(a, b, (((0,), (0,)), ((), ())), preferred_element_type=F32)


def _segment_rms_scale(x, seg_ref, width):
    hi, lo = _split2(x * x)
    seg = seg_ref[...]
    ms = (_dot(hi, seg) + _dot(lo, seg)) * (1.0 / width)
    return lax.rsqrt(ms + EPS)


def _in_proj_kernel(x_ref, g_ref, w_ref, fb_row_ref, *refs):
    u_ref, qk_ref, gv_ref, gg_ref, misc_ref, fq_ref, fk_ref, fv_ref, logf_ref = refs[2:]
    x = x_ref[...]
    ms = jnp.mean(x * x, axis=-1, keepdims=True)
    h = (x * lax.rsqrt(ms + EPS) * g_ref[...]).astype(BF16)
    for ref, c0 in ((u_ref, COL_U), (qk_ref, COL_QK), (gv_ref, COL_GV), (gg_ref, COL_GG),
                    (fq_ref, COL_FQ), (fk_ref, COL_FK), (fv_ref, COL_FV)):
        ref[...] = _dot(h, w_ref[:, c0:c0 + ref.shape[-1]]).reshape(ref.shape)
    misc = _dot(h, w_ref[:, COL_MISC:COL_MISC + LANES])
    misc_ref[...] = misc
    logf_ref[...] = _log_sigmoid(misc[:, MISC_FF:MISC_FF + FOX_HEADS] + fb_row_ref[...]).reshape(logf_ref.shape)


def _in_proj(x2d, g, w_pack, fb_row, tm, layer, depth, stacks):
    n = x2d.shape[0]
    _, nb, seq, _ = stacks[0].shape
    assert seq % tm == 0 or tm % seq == 0
    row = lambda w: pl.BlockSpec((tm, w), lambda i: (i, 0))
    if tm <= seq:
        per = seq // tm
        slab = lambda w: pl.BlockSpec((None, 1, tm, w), lambda i: (layer, i // per, i % per, 0))
    else:
        slab = lambda w: pl.BlockSpec((None, tm // seq, seq, w), lambda i: (layer, i, 0, 0))
    widths = (S5_WIDTH, 256, GLA_WIDTH, GLA_WIDTH, LANES, FOX_WIDTH, FOX_WIDTH)
    in_specs = [row(D_MODEL), _const_spec((1, D_MODEL)), _const_spec((D_MODEL, N_PACK)), _const_spec((1, FOX_HEADS))]
    args = [x2d, g, w_pack, fb_row]
    aliases = {len(args): len(widths), len(args) + 1: len(widths) + 1}
    in_specs += [pl.BlockSpec(memory_space=pl.ANY)] * 2
    args += list(stacks)
    return pl.pallas_call(
        _in_proj_kernel,
        grid=(n // tm,),
        in_specs=in_specs,
        out_specs=[row(w) for w in widths] + [slab(FOX_WIDTH), slab(FOX_HEADS)],
        out_shape=[jax.ShapeDtypeStruct((n, w), F32) for w in widths]
                  + [jax.ShapeDtypeStruct((depth, nb, seq, FOX_WIDTH), F32),
                     jax.ShapeDtypeStruct((depth, nb, seq, FOX_HEADS), F32)],
        input_output_aliases=aliases,
        compiler_params=_cparams(("parallel",)),
        name="in_proj",
    )(*args)


def _gelu_tanh(x):
    c = math.sqrt(2.0 / math.pi)
    return x * (0.5 * (1.0 + jnp.tanh(c * (x + 0.044715 * (x * x * x)))))


def _s5_kernel(u_ref, h0r_ref, h0i_ref, ar_ref, ai_ref, bmat_ref, cmat_ref, d_ref, gw_ref, gb_ref,
               o_ref, hr_out_ref, hi_out_ref, hbuf, hr_s, hi_s, *, tl, nb):
    i = pl.program_id(0)

    @pl.when(i == 0)
    def _():
        hr_s[...] = h0r_ref[...]
        hi_s[...] = h0i_ref[...]

    rows = tl * nb
    u = u_ref[...].reshape(rows, S5_WIDTH)
    ub = u.astype(BF16)
    sub = min(tl, S5_SUB_STEPS)
    srows = sub * nb
    re = slice(0, S5_GN)
    im = slice(S5_GN, 2 * S5_GN)

    def input_matmul(s):
        hbuf[s * srows:(s + 1) * srows, :] = _dot(ub[s * srows:(s + 1) * srows], bmat_ref[...])

    input_matmul(0)
    hr, hi = hr_s[...], hi_s[...]
    ys = []
    for s in range(tl // sub):
        if (s + 1) * sub < tl:
            input_matmul(s + 1)
        for t in range(s * sub, (s + 1) * sub):
            r = slice(t * nb, (t + 1) * nb)
            ar, ai = ar_ref[...], ai_ref[...]
            nhr = ar * hr - ai * hi + hbuf[r, re]
            nhi = ar * hi + ai * hr + hbuf[r, im]
            hbuf[r, re] = nhr
            hbuf[r, im] = nhi
            hr, hi = nhr, nhi
        ys.append(_dot(hbuf[s * srows:(s + 1) * srows, :].astype(BF16), cmat_ref[...]))
    hr_s[...] = hr
    hi_s[...] = hi

    y = (ys[0] if len(ys) == 1 else jnp.concatenate(ys, axis=0)) + d_ref[...] * u
    y = _gelu_tanh(y)
    z = _dot(y.astype(BF16), gw_ref[...]) + gb_ref[...]
    o_ref[...] = (y * _sigmoid(z)).reshape(tl, nb, S5_WIDTH)
    hr_out_ref[...] = hr_s[...]
    hi_out_ref[...] = hi_s[...]


def _s5(u_t, h0r, h0i, ar, ai, bmat, cmat, d, gw, gb, tl):
    seq, nb, _ = u_t.shape
    kern = functools.partial(_s5_kernel, tl=tl, nb=nb)
    state = _const_spec((nb, S5_GN))
    return pl.pallas_call(
        kern,
        grid=(seq // tl,),
        in_specs=[pl.BlockSpec((tl, nb, S5_WIDTH), lambda i: (i, 0, 0)), state, state,
                  _const_spec((1, S5_GN)), _const_spec((1, S5_GN)),
                  _const_spec((S5_WIDTH, 2 * S5_GN)), _const_spec((2 * S5_GN, S5_WIDTH)),
                  _const_spec((1, S5_WIDTH)), _const_spec((S5_WIDTH, S5_WIDTH)), _const_spec((1, S5_WIDTH))],
        out_specs=[pl.BlockSpec((tl, nb, S5_WIDTH), lambda i: (i, 0, 0)), state, state],
        out_shape=[jax.ShapeDtypeStruct((seq, nb, S5_WIDTH), F32),
                   jax.ShapeDtypeStruct((nb, S5_GN), F32), jax.ShapeDtypeStruct((nb, S5_GN), F32)],
        scratch_shapes=[pltpu.VMEM((tl * nb, 2 * S5_GN), F32),
                        pltpu.VMEM((nb, S5_GN), F32), pltpu.VMEM((nb, S5_GN), F32)],
        compiler_params=_cparams(("arbitrary",)),
        name="s5",
    )(u_t, h0r, h0i, ar, ai, bmat, cmat, d, gw, gb)


def _gla_kernel(qk_ref, v_ref, g_ref, misc_ref, a2_ref, ab_ref, ng_ref, s0_ref,
                tril_ref, mk_ref, mv_ref, ms_ref, causal_ref, seg_ref,
                o_ref, sfin_ref, s_scr, *, tl, chunk):
    i = pl.program_id(1)

    @pl.when(i == 0)
    def _():
        s_scr[...] = s0_ref[0]

    qk = qk_ref[0]
    q = qk[:, :GLA_KW] * (GLA_DK ** -0.5)
    k = qk[:, GLA_KW:]
    v = v_ref[0]
    la = _log_sigmoid(_dot(misc_ref[0].astype(BF16), a2_ref[...]) + ab_ref[...]) * (1.0 / GLA_TAU)

    tril = tril_ref[...]
    span = tril.shape[0]
    parts = _split3(la)
    b = jnp.concatenate([sum(_dot(tril, part[r0:r0 + span]) for part in parts) for r0 in range(0, tl, span)], axis=0)
    nchunk = tl // chunk
    b3 = b.reshape(nchunk, chunk, GLA_KW)
    b_last = b3[:, chunk - 1:chunk, :]
    b_end = (b_last - b3).reshape(tl, GLA_KW)
    q_dec = (q * jnp.exp(b)).astype(BF16)
    k_inv = k * jnp.exp(-b)
    k_end = (k * jnp.exp(b_end)).astype(BF16)
    e_last = jnp.exp(b_last)

    outs = []
    for c in range(nchunk):
        sl = slice(c * chunk, (c + 1) * chunk)
        qd = q_dec[sl]
        kbd = jnp.where(mk_ref[...] > 0, jnp.concatenate([k_inv[sl]] * GLA_HEADS, axis=0), 0.0).astype(BF16)
        att = jnp.where(causal_ref[...] > 0, _dot_nt(qd, kbd), 0.0)
        vc = v[sl]
        vbd = jnp.where(mv_ref[...] > 0, jnp.concatenate([vc] * GLA_HEADS, axis=0), 0.0).astype(BF16)
        s = s_scr[...]
        outs.append(_dot(att.astype(BF16), vbd) + _dot_nt(qd, s.astype(BF16)))
        upd = jnp.where(ms_ref[...] > 0, _dot_tn(vc.astype(BF16), k_end[sl]), 0.0)
        s_scr[...] = s * e_last[c] + upd
    o = outs[0] if nchunk == 1 else jnp.concatenate(outs, axis=0)

    o = o * _segment_rms_scale(o, seg_ref, GLA_DV) * ng_ref[...]
    g = g_ref[0]
    o_ref[0] = o * (g * _sigmoid(g))
    sfin_ref[0] = s_scr[...]


def _gla_masks(tl, chunk):
    t = np.arange(min(tl, GLA_CUMSUM_SPAN))
    tril = ((t[:, None] // chunk == t[None, :] // chunk) & (t[None, :] <= t[:, None])).astype(np.float32)
    r = np.arange(GLA_HEADS * chunk)
    mk = (r[:, None] // chunk == np.arange(GLA_KW)[None, :] // GLA_DK).astype(np.float32)
    mv = (r[:, None] // chunk == np.arange(GLA_WIDTH)[None, :] // GLA_DV).astype(np.float32)
    ms = (np.arange(GLA_WIDTH)[:, None] // GLA_DV == np.arange(GLA_KW)[None, :] // GLA_DK).astype(np.float32)
    causal = (np.arange(chunk)[:, None] >= (r[None, :] % chunk)).astype(np.float32)
    seg = (np.arange(GLA_WIDTH)[:, None] // GLA_DV == np.arange(GLA_WIDTH)[None, :] // GLA_DV).astype(np.float32)
    return (jnp.asarray(tril, BF16), jnp.asarray(mk), jnp.asarray(mv), jnp.asarray(ms), jnp.asarray(causal),
            jnp.asarray(seg, BF16))


def _gla(qk, v, g, misc, a2p, ab, ng, s0t, tl, chunk):
    nb, seq, _ = qk.shape
    kern = functools.partial(_gla_kernel, tl=tl, chunk=chunk)
    masks = _gla_masks(tl, chunk)
    tok = lambda w: pl.BlockSpec((1, tl, w), lambda b, i: (b, i, 0))
    st = pl.BlockSpec((1, GLA_WIDTH, GLA_KW), lambda b, i: (b, 0, 0))
    return pl.pallas_call(
        kern,
        grid=(nb, seq // tl),
        in_specs=[tok(256), tok(GLA_WIDTH), tok(GLA_WIDTH), tok(LANES),
                  _const_spec((LANES, GLA_KW)), _const_spec((1, GLA_KW)), _const_spec((1, GLA_WIDTH)), st]
                 + [_const_spec(m.shape) for m in masks],
        out_specs=[tok(GLA_WIDTH), st],
        out_shape=[jax.ShapeDtypeStruct((nb, seq, GLA_WIDTH), F32),
                   jax.ShapeDtypeStruct((nb, GLA_WIDTH, GLA_KW), F32)],
        scratch_shapes=[pltpu.VMEM((GLA_WIDTH, GLA_KW), F32)],
        compiler_params=_cparams(("parallel", "arbitrary")),
        name="gla",
    )(qk, v, g, misc, a2p, ab, ng, s0t, *masks)


FOX_KAUG = 2 * FOX_HD


def _fox_constants():
    sel_k = np.zeros((FOX_HEADS // 2, 2 * LANES, 2 * FOX_KAUG), np.float32)
    sel_q = np.zeros((2 * FOX_KAUG, LANES), np.float32)
    ones_q = np.zeros((2 * FOX_KAUG, 1), np.float32)
    for hh in range(2):
        for d in range(FOX_HD):
            sel_k[:, hh * FOX_HD + d, hh * FOX_KAUG + d] = 1.0
            sel_q[hh * FOX_KAUG + d, hh * FOX_HD + d] = 1.0
        for part in range(3):
            ones_q[hh * FOX_KAUG + FOX_HD + part, 0] = 1.0
            for pair in range(FOX_HEADS // 2):
                sel_k[pair, LANES + part * FOX_HEADS + 2 * pair + hh, hh * FOX_KAUG + FOX_HD + part] = 1.0
    t = np.arange(FOX_WIDTH)
    seg = (t[:, None] // FOX_HD == t[None, :] // FOX_HD).astype(np.float32)
    return jnp.asarray(sel_k, BF16), jnp.asarray(sel_q, BF16), jnp.asarray(ones_q), jnp.asarray(seg, BF16)


def _lower_ones(n):
    t = np.arange(n)
    return jnp.asarray((t[None, :] <= t[:, None]).astype(np.float32), BF16)


def _fox_kernel(*refs, tq, tqp, tk, seq, past, tkp):
    (fq_ref, fk_ref, fv_ref, lf_ref, qg_ref, kg_ref, seg_ref, selk_ref, selq_ref, onesq_ref, tril_ref) = refs[:11]
    refs = refs[11:]
    if past:
        pk_ref, pv_ref, plf_ref, trilp_ref = refs[:4]
        refs = refs[4:]
    refs = refs[1:]
    o_ref, krow_ref, kaug, vt, s_a, s_b, m_s, l_s, acc_s = refs[:9]
    if past:
        pkaug, pvt = refs[9:]
    i = pl.program_id(1)
    blk = min(tk, seq)

    def stage_block(k_bf, v_f32, logf, carry, tril, kaug_dst, vt_dst):
        n = logf.shape[0]
        logf128 = jnp.concatenate([logf, jnp.zeros((n, LANES - FOX_HEADS), F32)], axis=1)
        c = sum(_dot(tril, part) for part in _split3(logf128)) + carry
        hi, mid, lo = (part.astype(F32) for part in _split3(c * (-LOG2E)))
        packed = (hi + pltpu.roll(mid, FOX_HEADS, 1) + pltpu.roll(lo, 2 * FOX_HEADS, 1)).astype(BF16)
        for pair in range(FOX_HEADS // 2):
            x = jnp.concatenate([k_bf[:, pair * LANES:(pair + 1) * LANES], packed], axis=1)
            kaug_dst[0:n, pair * 2 * FOX_KAUG:(pair + 1) * 2 * FOX_KAUG] = _dot(x, selk_ref[pair]).astype(BF16)
        vt_dst[:, 0:n] = v_f32.T.astype(BF16)
        return c[n - 1:n, :]

    @pl.when(i == 0)
    def _prologue():
        carry = jnp.zeros((1, LANES), F32)
        if past:
            for jb in range(past // tkp):
                r = slice(jb * tkp, (jb + 1) * tkp)
                carry = stage_block(pk_ref[0, r, :].astype(BF16), pv_ref[0, r, :], plf_ref[0, r, :],
                                    carry, trilp_ref[...], pkaug.at[jb], pvt.at[jb])
        if blk < tk:
            kaug[...] = jnp.zeros_like(kaug)
            vt[...] = jnp.zeros_like(vt)
        for jb in range(max(seq // tk, 1)):
            r = slice(jb * tk, jb * tk + blk)
            kf = fk_ref[0, r, :]
            kn = kf * _segment_rms_scale(kf, seg_ref, FOX_HD) * kg_ref[...]
            krow_ref[0, r, :] = kn
            carry = stage_block(kn.astype(BF16), fv_ref[0, r, :], lf_ref[0, r, :],
                                carry, tril_ref[...], kaug.at[jb], vt.at[jb])

    qf = fq_ref[0]
    qn = (qf * _segment_rms_scale(qf, seg_ref, FOX_HD) * qg_ref[...] * (FOX_HD ** -0.5 * LOG2E)).astype(BF16)
    if tqp > tq:
        qn = jnp.concatenate([qn, jnp.zeros((tqp - tq, FOX_WIDTH), BF16)], axis=0)
    qt_all = jnp.concatenate(
        [(_dot_nt(selq_ref[...], qn[:, pair * LANES:(pair + 1) * LANES]) + onesq_ref[...]).astype(BF16)
         for pair in range(FOX_HEADS // 2)], axis=0)
    key_i = lax.broadcasted_iota(jnp.int32, (tk, tqp), 0)
    qry_i = lax.broadcasted_iota(jnp.int32, (tk, tqp), 1)
    diag_ok = key_i <= qry_i

    klanes = [slice(h * FOX_KAUG, (h + 1) * FOX_KAUG) for h in range(FOX_HEADS)]
    vrows = [slice(h * FOX_HD, (h + 1) * FOX_HD) for h in range(FOX_HEADS)]
    qts = [qt_all[klanes[h], :] for h in range(FOX_HEADS)]

    def softmax_pv(h, s, v_t, mask):
        if mask is not None:
            s = jnp.where(mask, s, NEG_BIG)
        m = m_s[h]
        m_new = jnp.maximum(m, jnp.max(s, axis=0, keepdims=True))
        p = jnp.exp2(s - m_new)
        alpha = jnp.exp2(m - m_new)
        m_s[h] = m_new
        l_s[h] = alpha * l_s[h] + jnp.sum(p, axis=0, keepdims=True)
        acc_s[h] = alpha * acc_s[h] + _dot(v_t, p.astype(BF16))

    def block_step(j, src, dst):
        for h in range(FOX_HEADS):
            dst[h, 0:tk, :] = _dot(kaug[j + 1, :, klanes[h]], qts[h])
            softmax_pv(h, src[h, 0:tk, :], vt[j, vrows[h], :], None)

    for h in range(FOX_HEADS):
        m_s[h] = jnp.full((1, tqp), NEG_BIG, F32)
        l_s[h] = jnp.zeros((1, tqp), F32)
        acc_s[h] = jnp.zeros((FOX_HD, tqp), F32)
    if past:
        npb = past // tkp
        for h in range(FOX_HEADS):
            s_a[h, 0:tkp, :] = _dot(pkaug[0, :, klanes[h]], qts[h])
        src, dst = s_a, s_b
        for jb in range(npb):
            for h in range(FOX_HEADS):
                if jb + 1 < npb:
                    dst[h, 0:tkp, :] = _dot(pkaug[jb + 1, :, klanes[h]], qts[h])
                else:
                    dst[h, 0:tk, :] = _dot(kaug[0, :, klanes[h]], qts[h])
                softmax_pv(h, src[h, 0:tkp, :], pvt[jb, vrows[h], :], None)
            src, dst = dst, src
        if src is s_b:
            s_a[:, 0:tk, :] = s_b[:, 0:tk, :]
    else:
        for h in range(FOX_HEADS):
            s_a[h, 0:tk, :] = _dot(kaug[0, :, klanes[h]], qts[h])

    def body(jj, carry):
        block_step(2 * jj, s_a, s_b)
        block_step(2 * jj + 1, s_b, s_a)
        return carry

    lax.fori_loop(0, i // 2, body, 0)

    @pl.when(i % 2 == 1)
    def _odd_tail():
        block_step(i - 1, s_a, s_b)
        s_a[:, 0:tk, :] = s_b[:, 0:tk, :]

    outs = []
    for h in range(FOX_HEADS):
        softmax_pv(h, s_a[h, 0:tk, :], vt[i, vrows[h], :], diag_ok)
        outs.append(acc_s[h] / l_s[h])
    o_t = jnp.concatenate(outs, axis=0)
    o_ref[0] = o_t.T[0:tq, :]


def _fox(fq, fk, fv, logf, qg, kg, tq, tk, layer, krow_stack, past_k=None, past_v=None, past_logf=None):
    nb, seq, _ = fq.shape
    depth = fv.shape[0]
    nq = seq // tq
    tqp = max(tq, LANES)
    assert tqp == tk and (tq == tk or nq == 1)
    nkb = max(seq // tk, 1)
    past = 0 if past_k is None else past_k.shape[1]
    tkp = min(past, 256) if past else 0
    sel_k, sel_q, ones_q, seg = _fox_constants()
    kern = functools.partial(_fox_kernel, tq=tq, tqp=tqp, tk=tk, seq=seq, past=past, tkp=tkp)
    full = lambda rows, w: pl.BlockSpec((1, rows, w), lambda b, i: (b, 0, 0))
    slab = lambda w: pl.BlockSpec((None, 1, seq, w), lambda b, i: (layer, b, 0, 0))
    consts = [qg, kg, seg, sel_k, sel_q, ones_q, _lower_ones(min(tk, seq))]
    in_specs = [pl.BlockSpec((1, tq, FOX_WIDTH), lambda b, i: (b, i, 0)), full(seq, FOX_WIDTH), slab(FOX_WIDTH),
                slab(FOX_HEADS)] + [_const_spec(c.shape) for c in consts]
    args = [fq, fk, fv, logf] + consts
    scratch = [pltpu.VMEM((nkb, tk, FOX_HEADS * FOX_KAUG), BF16), pltpu.VMEM((nkb, FOX_WIDTH, tk), BF16),
               pltpu.VMEM((FOX_HEADS, max(tk, tkp), tqp), F32), pltpu.VMEM((FOX_HEADS, max(tk, tkp), tqp), F32),
               pltpu.VMEM((FOX_HEADS, 1, tqp), F32), pltpu.VMEM((FOX_HEADS, 1, tqp), F32),
               pltpu.VMEM((FOX_HEADS, FOX_HD, tqp), F32)]
    if past:
        in_specs += [full(past, FOX_WIDTH), full(past, FOX_WIDTH), full(past, FOX_HEADS), _const_spec((tkp, tkp))]
        args += [past_k, past_v, past_logf, _lower_ones(tkp)]
        scratch += [pltpu.VMEM((past // tkp, tkp, FOX_HEADS * FOX_KAUG), BF16),
                    pltpu.VMEM((past // tkp, FOX_WIDTH, tkp), BF16)]
    aliases = {len(args): 1}
    in_specs += [pl.BlockSpec(memory_space=pl.ANY)]
    args += [krow_stack]
    return pl.pallas_call(
        kern,
        grid=(nb, nq),
        in_specs=in_specs,
        out_specs=[pl.BlockSpec((1, tq, FOX_WIDTH), lambda b, i: (b, i, 0)), slab(FOX_WIDTH)],
        out_shape=[jax.ShapeDtypeStruct((nb, seq, FOX_WIDTH), F32),
                   jax.ShapeDtypeStruct((depth, nb, seq, FOX_WIDTH), F32)],
        input_output_aliases=aliases,
        scratch_shapes=scratch,
        compiler_params=_cparams(("parallel", "arbitrary")),
        name="fox",
    )(*args)


def _out_mlp_kernel(x_ref, s5_ref, gla_ref, fox_ref, wo_ref, g2_ref, wu_ref, wd_ref, y_ref, up_s):
    x1 = (x_ref[...]
          + _dot(s5_ref[...].astype(BF16), wo_ref[0:S5_WIDTH, :])
          + _dot(gla_ref[...].astype(BF16), wo_ref[S5_WIDTH:S5_WIDTH + GLA_WIDTH, :])
          + _dot(fox_ref[...].astype(BF16), wo_ref[S5_WIDTH + GLA_WIDTH:, :]))
    ms = jnp.mean(x1 * x1, axis=-1, keepdims=True)
    h2 = (x1 * lax.rsqrt(ms + EPS) * g2_ref[...]).astype(BF16)
    for c in range(D_FF // FF_CHUNK):
        cols = slice(c * FF_CHUNK, (c + 1) * FF_CHUNK)
        up = jnp.maximum(_dot(h2, wu_ref[:, cols]), 0.0)
        up_s[:, cols] = (up * up).astype(BF16)
    y_ref[...] = x1 + _dot(up_s[...], wd_ref[...])


def _out_mlp(x2d, s5, gla, fox, wo, g2, wu, wd, tm):
    n = x2d.shape[0]
    row = lambda w: pl.BlockSpec((tm, w), lambda i: (i, 0))
    resident = pl.BlockSpec(memory_space=pltpu.VMEM)
    return pl.pallas_call(
        _out_mlp_kernel,
        grid=(n // tm,),
        in_specs=[row(D_MODEL), row(S5_WIDTH), row(GLA_WIDTH), row(FOX_WIDTH),
                  resident, _const_spec((1, D_MODEL)), resident, resident],
        out_specs=row(D_MODEL),
        out_shape=jax.ShapeDtypeStruct((n, D_MODEL), F32),
        scratch_shapes=[pltpu.VMEM((tm, D_FF), BF16)],
        compiler_params=_cparams(("parallel",)),
        name="out_mlp",
    )(x2d, s5, gla, fox, wo, g2, wu, wd)


def _pack_params(norm1_g, w_in, s5_a_re, s5_a_im, s5_log_dt, s5_b_re, s5_b_im, s5_c_re, s5_c_im, s5_d,
                 s5_glu_w, s5_glu_b, gla_a2, gla_a_bias, gla_norm_g, fox_q_norm_g, fox_k_norm_g, fox_f_bias,
                 w_out, norm2_g, w_up, w_down):
    depth = w_in.shape[0]
    o_ga = 1024
    o_fq = o_ga + GLA_RANK
    o_ff = o_fq + 3 * FOX_WIDTH
    w_ff = w_in[:, :, o_ff:o_ff + FOX_HEADS]
    pad = jnp.zeros((depth, D_MODEL, LANES - FOX_HEADS - GLA_RANK), F32)
    w_pack = jnp.concatenate([w_in[:, :, :o_ga], w_ff, w_in[:, :, o_ga:o_fq], pad, w_in[:, :, o_fq:o_ff]],
                             axis=-1).astype(BF16)
    a2p = jnp.zeros((depth, LANES, GLA_KW), F32).at[:, MISC_GA:MISC_GA + GLA_RANK, :].set(gla_a2).astype(BF16)

    dt = jnp.exp(s5_log_dt)[..., None]
    mag = jnp.exp(dt * s5_a_re)
    abar_re = mag * jnp.cos(dt * s5_a_im)
    abar_im = mag * jnp.sin(dt * s5_a_im)
    den = s5_a_re * s5_a_re + s5_a_im * s5_a_im
    zr = ((abar_re - 1.0) * s5_a_re + abar_im * s5_a_im) / den
    zi = (abar_im * s5_a_re - (abar_re - 1.0) * s5_a_im) / den
    bb_re = zr[..., None] * s5_b_re - zi[..., None] * s5_b_im
    bb_im = zr[..., None] * s5_b_im + zi[..., None] * s5_b_re
    eye = jnp.eye(S5_GROUPS, dtype=F32)
    to_b = lambda bb: jnp.einsum('lgnp,gh->lgphn', bb, eye).reshape(depth, S5_WIDTH, S5_GN)
    bmat = jnp.concatenate([to_b(bb_re), to_b(bb_im)], axis=-1).astype(BF16)
    to_c = lambda cc: jnp.einsum('lgpn,gh->lgnhp', cc, eye).reshape(depth, S5_GN, S5_WIDTH)
    cmat = jnp.concatenate([to_c(s5_c_re), -to_c(s5_c_im)], axis=1).astype(BF16)

    return dict(
        norm1_g=norm1_g[:, None, :], w_pack=w_pack, fb_row=fox_f_bias[:, None, :],
        abar_re=abar_re.reshape(depth, 1, S5_GN), abar_im=abar_im.reshape(depth, 1, S5_GN),
        bmat=bmat, cmat=cmat, s5_d=s5_d.reshape(depth, 1, S5_WIDTH),
        glu_w=s5_glu_w.astype(BF16), glu_b=s5_glu_b[:, None, :],
        a2p=a2p, ab=gla_a_bias[:, None, :], ng=jnp.tile(gla_norm_g, (1, GLA_HEADS))[:, None, :],
        qg=jnp.tile(fox_q_norm_g, (1, FOX_HEADS))[:, None, :], kg=jnp.tile(fox_k_norm_g, (1, FOX_HEADS))[:, None, :],
        w_out=w_out.astype(BF16), norm2_g=norm2_g[:, None, :], w_up=w_up.astype(BF16), w_down=w_down.astype(BF16),
    )


def _gla_state_to_kernel(s):
    eye = jnp.eye(GLA_HEADS, dtype=s.dtype)
    return jnp.einsum('bhkv,hg->bhvgk', s, eye).reshape(s.shape[0], GLA_WIDTH, GLA_KW)


def _gla_state_from_kernel(st):
    nb = st.shape[0]
    s5d = st.reshape(nb, GLA_HEADS, GLA_DV, GLA_HEADS, GLA_DK)
    diag = jnp.stack([s5d[:, h, :, h, :] for h in range(GLA_HEADS)], axis=1)
    return jnp.swapaxes(diag, 2, 3)


def _tiles(nb, seq):
    n = nb * seq
    fox_tq = min(256, seq)
    return dict(in_rows=min(1024, n), out_rows=min(512, n), s5_tl=min(64, seq), gla_tl=min(1024, seq),
                gla_chunk=min(GLA_CHUNK, seq), fox_tq=fox_tq, fox_tk=max(fox_tq, LANES))


def _layer(x, p, l, depth, stacks, s5_h0r, s5_h0i, gla_s0, past_k, past_v, past_logf):
    nb, seq, _ = x.shape
    n = nb * seq
    t = _tiles(nb, seq)
    s5_tl, gla_tl, gla_chunk, fox_tq, fox_tk = t['s5_tl'], t['gla_tl'], t['gla_chunk'], t['fox_tq'], t['fox_tk']
    x2d = x.reshape(n, D_MODEL)
    u, qk, gv, gg, misc, fq, fk, fv_stack, logf_stack = _in_proj(
        x2d, p['norm1_g'][l], p['w_pack'][l], p['fb_row'][l], t['in_rows'], l, depth,
        stacks[:2])

    u_t = jnp.swapaxes(u.reshape(nb, seq, S5_WIDTH), 0, 1)
    o_s5_t, hr, hi = _s5(u_t, s5_h0r, s5_h0i, p['abar_re'][l], p['abar_im'][l], p['bmat'][l], p['cmat'][l],
                         p['s5_d'][l], p['glu_w'][l], p['glu_b'][l], s5_tl)
    o_s5 = jnp.swapaxes(o_s5_t, 0, 1).reshape(n, S5_WIDTH)

    b3 = lambda a: a.reshape(nb, seq, a.shape[-1])
    o_gla, s_t = _gla(b3(qk), b3(gv), b3(gg), b3(misc), p['a2p'][l], p['ab'][l], p['ng'][l],
                      _gla_state_to_kernel(gla_s0), gla_tl, gla_chunk)

    b4 = lambda a: a.reshape(depth, nb, seq, a.shape[-1])
    past = ()
    if past_k is not None:
        plen = past_k.shape[1]
        past = (past_k.reshape(nb, plen, FOX_WIDTH), past_v.reshape(nb, plen, FOX_WIDTH), past_logf)
    o_fox, krow_stack = _fox(b3(fq), b3(fk), b4(fv_stack), b4(logf_stack), p['qg'][l], p['kg'][l], fox_tq, fox_tk,
                             l, b4(stacks[2]), *past)

    y = _out_mlp(x2d, o_s5, o_gla.reshape(n, GLA_WIDTH), o_fox.reshape(n, FOX_WIDTH),
                 p['w_out'][l], p['norm2_g'][l], p['w_up'][l], p['w_down'][l], t['out_rows'])
    states = (hr.reshape(nb, S5_GROUPS, S5_STATE), hi.reshape(nb, S5_GROUPS, S5_STATE), _gla_state_from_kernel(s_t))
    return y.reshape(nb, seq, D_MODEL), states, (fv_stack, logf_stack, krow_stack)


def kernel(x_prompt, x_sample, state_s5_re, state_s5_im, state_gla, cache_fox_k, cache_fox_v, cache_fox_logf, norm1_g, w_in, s5_a_re, s5_a_im, s5_log_dt, s5_b_re, s5_b_im, s5_c_re, s5_c_im, s5_d, s5_glu_w, s5_glu_b, gla_a2, gla_a_bias, gla_norm_g, fox_q_norm_g, fox_k_norm_g, fox_f_bias, w_out, norm2_g, w_up, w_down):
    depth = w_in.shape[0]
    p = _pack_params(norm1_g, w_in, s5_a_re, s5_a_im, s5_log_dt, s5_b_re, s5_b_im, s5_c_re, s5_c_im, s5_d,
                     s5_glu_w, s5_glu_b, gla_a2, gla_a_bias, gla_norm_g, fox_q_norm_g, fox_k_norm_g, fox_f_bias,
                     w_out, norm2_g, w_up, w_down)
    bp, lp, _ = x_prompt.shape
    bs, ls, _ = x_sample.shape

    yp, ys = x_prompt, x_sample
    p_st = [[] for _ in range(3)]
    s_st = [[] for _ in range(3)]
    new_stacks = lambda nb, seq: (jnp.zeros((depth, nb, seq, FOX_WIDTH), F32),
                                  jnp.zeros((depth, nb, seq, FOX_HEADS), F32),
                                  jnp.zeros((depth, nb, seq, FOX_WIDTH), F32))
    p_stacks, s_stacks = new_stacks(bp, lp), new_stacks(bs, ls)
    z_s5 = jnp.zeros((bp, S5_GN), F32)
    z_gla = jnp.zeros((bp, GLA_HEADS, GLA_DK, GLA_DV), F32)
    for l in range(depth):
        yp, stp, p_stacks = _layer(yp, p, l, depth, p_stacks, z_s5, z_s5, z_gla, None, None, None)
        ys, sts, s_stacks = _layer(ys, p, l, depth, s_stacks,
                                   state_s5_re[l].reshape(bs, S5_GN), state_s5_im[l].reshape(bs, S5_GN),
                                   state_gla[l], cache_fox_k[l], cache_fox_v[l], cache_fox_logf[l])
        for j in range(3):
            p_st[j].append(stp[j])
            s_st[j].append(sts[j])

    def fox_states(stacks, nb, seq):
        fv, logf, krow = stacks
        return (krow.reshape(depth, nb, seq, FOX_HEADS, FOX_HD), fv.reshape(depth, nb, seq, FOX_HEADS, FOX_HD),
                logf.reshape(depth, nb, seq, FOX_HEADS))

    return (yp, ys, *[jnp.stack(a) for a in p_st], *fox_states(p_stacks, bp, lp),
            *[jnp.stack(a) for a in s_st], *fox_states(s_stacks, bs, ls))
```

```python
import functools
import math

import jax
import jax.numpy as jnp
import numpy as np
from jax import lax
from jax.experimental import pallas as pl
from jax.experimental.pallas import tpu as pltpu

F32 = jnp.float32
BF16 = jnp.bfloat16

D_MODEL = 1024
EPS = 1e-6
LANES = 128

S5_WIDTH = 256
S5_GROUPS = 16
S5_GROUP = 16
S5_STATE = 64
S5_GN = S5_GROUPS * S5_STATE
S5_SUB_STEPS = 16

GLA_WIDTH = 256
GLA_HEADS = 4
GLA_DK = 32
GLA_DV = 64
GLA_KW = 128
GLA_RANK = 16
GLA_TAU = 16.0
GLA_CHUNK = 64
GLA_CUMSUM_SPAN = 256

FOX_WIDTH = 512
FOX_HD = 64
FOX_HEADS = 8
NEG_BIG = -1e30
LOG2E = math.log2(math.e)

D_FF = 4096
FF_CHUNK = 1024

COL_U = 0
COL_QK = 256
COL_GV = 512
COL_GG = 768
COL_MISC = 1024
COL_FQ = 1152
COL_FK = 1664
COL_FV = 2176
N_PACK = 2688
MISC_FF = 0
MISC_GA = 8

VMEM_LIMIT = 56 * 1024 * 1024


def _cparams(sem):
    return pltpu.CompilerParams(dimension_semantics=sem, vmem_limit_bytes=VMEM_LIMIT)


def _const_spec(shape):
    nd = len(shape)
    return pl.BlockSpec(shape, lambda *_: (0,) * nd)


def _log_sigmoid(x):
    return jnp.minimum(x, 0.0) - jnp.log1p(jnp.exp(-jnp.abs(x)))


def _sigmoid(x):
    return 1.0 / (1.0 + jnp.exp(-x))


def _split2(x):
    hi = x.astype(BF16)
    lo = (x - hi.astype(F32)).astype(BF16)
    return hi, lo


def _split3(x):
    hi = x.astype(BF16)
    r = x - hi.astype(F32)
    mid = r.astype(BF16)
    lo = (r - mid.astype(F32)).astype(BF16)
    return hi, mid, lo


def _dot(a, b):
    return jnp.dot(a, b, preferred_element_type=F32)


def _dot_nt(a, b):
    return lax.dot_general(a, b, (((1,), (1,)), ((), ())), preferred_element_type=F32)


def _dot_tn(a, b):
    return lax.dot_general(a, b, (((0,), (0,)), ((), ())), preferred_element_type=F32)


def _segment_rms_scale(x, seg_ref, width):
    hi, lo = _split2(x * x)
    seg = seg_ref[...]
    ms = (_dot(hi, seg) + _dot(lo, seg)) * (1.0 / width)
    return lax.rsqrt(ms + EPS)


def _in_proj_kernel(x_ref, g_ref, w_ref, fb_row_ref, *refs):
    u_ref, qk_ref, gv_ref, gg_ref, misc_ref, fq_ref, fk_ref, fv_ref, logf_ref = refs[2:]
    x = x_ref[...]
    ms = jnp.mean(x * x, axis=-1, keepdims=True)
    h = (x * lax.rsqrt(ms + EPS) * g_ref[...]).astype(BF16)
    for ref, c0 in ((u_ref, COL_U), (qk_ref, COL_QK), (gv_ref, COL_GV), (gg_ref, COL_GG),
                    (fq_ref, COL_FQ), (fk_ref, COL_FK), (fv_ref, COL_FV)):
        ref[...] = _dot(h, w_ref[:, c0:c0 + ref.shape[-1]]).reshape(ref.shape)
    misc = _dot(h, w_ref[:, COL_MISC:COL_MISC + LANES])
    misc_ref[...] = misc
    logf_ref[...] = _log_sigmoid(misc[:, MISC_FF:MISC_FF + FOX_HEADS] + fb_row_ref[...]).reshape(logf_ref.shape)


def _in_proj(x2d, g, w_pack, fb_row, tm, layer, depth, stacks):
    n = x2d.shape[0]
    _, nb, seq, _ = stacks[0].shape
    assert seq % tm == 0 or tm % seq == 0
    row = lambda w: pl.BlockSpec((tm, w), lambda i: (i, 0))
    if tm <= seq:
        per = seq // tm
        slab = lambda w: pl.BlockSpec((None, 1, tm, w), lambda i: (layer, i // per, i % per, 0))
    else:
        slab = lambda w: pl.BlockSpec((None, tm // seq, seq, w), lambda i: (layer, i, 0, 0))
    widths = (S5_WIDTH, 256, GLA_WIDTH, GLA_WIDTH, LANES, FOX_WIDTH, FOX_WIDTH)
    in_specs = [row(D_MODEL), _const_spec((1, D_MODEL)), _const_spec((D_MODEL, N_PACK)), _const_spec((1, FOX_HEADS))]
    args = [x2d, g, w_pack, fb_row]
    aliases = {len(args): len(widths), len(args) + 1: len(widths) + 1}
    in_specs += [pl.BlockSpec(memory_space=pl.ANY)] * 2
    args += list(stacks)
    return pl.pallas_call(
        _in_proj_kernel,
        grid=(n // tm,),
        in_specs=in_specs,
        out_specs=[row(w) for w in widths] + [slab(FOX_WIDTH), slab(FOX_HEADS)],
        out_shape=[jax.ShapeDtypeStruct((n, w), F32) for w in widths]
                  + [jax.ShapeDtypeStruct((depth, nb, seq, FOX_WIDTH), F32),
                     jax.ShapeDtypeStruct((depth, nb, seq, FOX_HEADS), F32)],
        input_output_aliases=aliases,
        compiler_params=_cparams(("parallel",)),
        name="in_proj",
    )(*args)


def _gelu_tanh(x):
    c = math.sqrt(2.0 / math.pi)
    return x * (0.5 * (1.0 + jnp.tanh(c * (x + 0.044715 * (x * x * x)))))


def _s5_kernel(u_ref, h0r_ref, h0i_ref, ar_ref, ai_ref, bmat_ref, cmat_ref, d_ref, gw_ref, gb_ref,
               o_ref, hr_out_ref, hi_out_ref, hbuf, hr_s, hi_s, *, tl, nb):
    i = pl.program_id(0)

    @pl.when(i == 0)
    def _():
        hr_s[...] = h0r_ref[...]
        hi_s[...] = h0i_ref[...]

    rows = tl * nb
    u = u_ref[...].reshape(rows, S5_WIDTH)
    ub = u.astype(BF16)
    sub = min(tl, S5_SUB_STEPS)
    srows = sub * nb
    re = slice(0, S5_GN)
    im = slice(S5_GN, 2 * S5_GN)

    def input_matmul(s):
        hbuf[s * srows:(s + 1) * srows, :] = _dot(ub[s * srows:(s + 1) * srows], bmat_ref[...])

    input_matmul(0)
    hr, hi = hr_s[...], hi_s[...]
    ys = []
    for s in range(tl // sub):
        if (s + 1) * sub < tl:
            input_matmul(s + 1)
        for t in range(s * sub, (s + 1) * sub):
            r = slice(t * nb, (t + 1) * nb)
            ar, ai = ar_ref[...], ai_ref[...]
            nhr = ar * hr - ai * hi + hbuf[r, re]
            nhi = ar * hi + ai * hr + hbuf[r, im]
            hbuf[r, re] = nhr
            hbuf[r, im] = nhi
            hr, hi = nhr, nhi
        ys.append(_dot(hbuf[s * srows:(s + 1) * srows, :].astype(BF16), cmat_ref[...]))
    hr_s[...] = hr
    hi_s[...] = hi

    y = (ys[0] if len(ys) == 1 else jnp.concatenate(ys, axis=0)) + d_ref[...] * u
    y = _gelu_tanh(y)
    z = _dot(y.astype(BF16), gw_ref[...]) + gb_ref[...]
    o_ref[...] = (y * _sigmoid(z)).reshape(tl, nb, S5_WIDTH)
    hr_out_ref[...] = hr_s[...]
    hi_out_ref[...] = hi_s[...]


def _s5(u_t, h0r, h0i, ar, ai, bmat, cmat, d, gw, gb, tl):
    seq, nb, _ = u_t.shape
    kern = functools.partial(_s5_kernel, tl=tl, nb=nb)
    state = _const_spec((nb, S5_GN))
    return pl.pallas_call(
        kern,
        grid=(seq // tl,),
        in_specs=[pl.BlockSpec((tl, nb, S5_WIDTH), lambda i: (i, 0, 0)), state, state,
                  _const_spec((1, S5_GN)), _const_spec((1, S5_GN)),
                  _const_spec((S5_WIDTH, 2 * S5_GN)), _const_spec((2 * S5_GN, S5_WIDTH)),
                  _const_spec((1, S5_WIDTH)), _const_spec((S5_WIDTH, S5_WIDTH)), _const_spec((1, S5_WIDTH))],
        out_specs=[pl.BlockSpec((tl, nb, S5_WIDTH), lambda i: (i, 0, 0)), state, state],
        out_shape=[jax.ShapeDtypeStruct((seq, nb, S5_WIDTH), F32),
                   jax.ShapeDtypeStruct((nb, S5_GN), F32), jax.ShapeDtypeStruct((nb, S5_GN), F32)],
        scratch_shapes=[pltpu.VMEM((tl * nb, 2 * S5_GN), F32),
                        pltpu.VMEM((nb, S5_GN), F32), pltpu.VMEM((nb, S5_GN), F32)],
        compiler_params=_cparams(("arbitrary",)),
        name="s5",
    )(u_t, h0r, h0i, ar, ai, bmat, cmat, d, gw, gb)


def _gla_kernel(qk_ref, v_ref, g_ref, misc_ref, a2_ref, ab_ref, ng_ref, s0_ref,
                tril_ref, mk_ref, mv_ref, ms_ref, causal_ref, seg_ref,
                o_ref, sfin_ref, s_scr, *, tl, chunk):
    i = pl.program_id(1)

    @pl.when(i == 0)
    def _():
        s_scr[...] = s0_ref[0]

    qk = qk_ref[0]
    q = qk[:, :GLA_KW] * (GLA_DK ** -0.5)
    k = qk[:, GLA_KW:]
    v = v_ref[0]
    la = _log_sigmoid(_dot(misc_ref[0].astype(BF16), a2_ref[...]) + ab_ref[...]) * (1.0 / GLA_TAU)

    tril = tril_ref[...]
    span = tril.shape[0]
    parts = _split3(la)
    b = jnp.concatenate([sum(_dot(tril, part[r0:r0 + span]) for part in parts) for r0 in range(0, tl, span)], axis=0)
    nchunk = tl // chunk
    b3 = b.reshape(nchunk, chunk, GLA_KW)
    b_last = b3[:, chunk - 1:chunk, :]
    b_end = (b_last - b3).reshape(tl, GLA_KW)
    q_dec = (q * jnp.exp(b)).astype(BF16)
    k_inv = k * jnp.exp(-b)
    k_end = (k * jnp.exp(b_end)).astype(BF16)
    e_last = jnp.exp(b_last)

    outs = []
    for c in range(nchunk):
        sl = slice(c * chunk, (c + 1) * chunk)
        qd = q_dec[sl]
        kbd = jnp.where(mk_ref[...] > 0, jnp.concatenate([k_inv[sl]] * GLA_HEADS, axis=0), 0.0).astype(BF16)
        att = jnp.where(causal_ref[...] > 0, _dot_nt(qd, kbd), 0.0)
        vc = v[sl]
        vbd = jnp.where(mv_ref[...] > 0, jnp.concatenate([vc] * GLA_HEADS, axis=0), 0.0).astype(BF16)
        s = s_scr[...]
        outs.append(_dot(att.astype(BF16), vbd) + _dot_nt(qd, s.astype(BF16)))
        upd = jnp.where(ms_ref[...] > 0, _dot_tn(vc.astype(BF16), k_end[sl]), 0.0)
        s_scr[...] = s * e_last[c] + upd
    o = outs[0] if nchunk == 1 else jnp.concatenate(outs, axis=0)

    o = o * _segment_rms_scale(o, seg_ref, GLA_DV) * ng_ref[...]
    g = g_ref[0]
    o_ref[0] = o * (g * _sigmoid(g))
    sfin_ref[0] = s_scr[...]


def _gla_masks(tl, chunk):
    t = np.arange(min(tl, GLA_CUMSUM_SPAN))
    tril = ((t[:, None] // chunk == t[None, :] // chunk) & (t[None, :] <= t[:, None])).astype(np.float32)
    r = np.arange(GLA_HEADS * chunk)
    mk = (r[:, None] // chunk == np.arange(GLA_KW)[None, :] // GLA_DK).astype(np.float32)
    mv = (r[:, None] // chunk == np.arange(GLA_WIDTH)[None, :] // GLA_DV).astype(np.float32)
    ms = (np.arange(GLA_WIDTH)[:, None] // GLA_DV == np.arange(GLA_KW)[None, :] // GLA_DK).astype(np.float32)
    causal = (np.arange(chunk)[:, None] >= (r[None, :] % chunk)).astype(np.float32)
    seg = (np.arange(GLA_WIDTH)[:, None] // GLA_DV == np.arange(GLA_WIDTH)[None, :] // GLA_DV).astype(np.float32)
    return (jnp.asarray(tril, BF16), jnp.asarray(mk), jnp.asarray(mv), jnp.asarray(ms), jnp.asarray(causal),
            jnp.asarray(seg, BF16))


def _gla(qk, v, g, misc, a2p, ab, ng, s0t, tl, chunk):
    nb, seq, _ = qk.shape
    kern = functools.partial(_gla_kernel, tl=tl, chunk=chunk)
    masks = _gla_masks(tl, chunk)
    tok = lambda w: pl.BlockSpec((1, tl, w), lambda b, i: (b, i, 0))
    st = pl.BlockSpec((1, GLA_WIDTH, GLA_KW), lambda b, i: (b, 0, 0))
    return pl.pallas_call(
        kern,
        grid=(nb, seq // tl),
        in_specs=[tok(256), tok(GLA_WIDTH), tok(GLA_WIDTH), tok(LANES),
                  _const_spec((LANES, GLA_KW)), _const_spec((1, GLA_KW)), _const_spec((1, GLA_WIDTH)), st]
                 + [_const_spec(m.shape) for m in masks],
        out_specs=[tok(GLA_WIDTH), st],
        out_shape=[jax.ShapeDtypeStruct((nb, seq, GLA_WIDTH), F32),
                   jax.ShapeDtypeStruct((nb, GLA_WIDTH, GLA_KW), F32)],
        scratch_shapes=[pltpu.VMEM((GLA_WIDTH, GLA_KW), F32)],
        compiler_params=_cparams(("parallel", "arbitrary")),
        name="gla",
    )(qk, v, g, misc, a2p, ab, ng, s0t, *masks)


FOX_KAUG = 2 * FOX_HD


def _fox_constants():
    sel_k = np.zeros((FOX_HEADS // 2, 2 * LANES, 2 * FOX_KAUG), np.float32)
    sel_q = np.zeros((2 * FOX_KAUG, LANES), np.float32)
    ones_q = np.zeros((2 * FOX_KAUG, 1), np.float32)
    for hh in range(2):
        for d in range(FOX_HD):
            sel_k[:, hh * FOX_HD + d, hh * FOX_KAUG + d] = 1.0
            sel_q[hh * FOX_KAUG + d, hh * FOX_HD + d] = 1.0
        for part in range(3):
            ones_q[hh * FOX_KAUG + FOX_HD + part, 0] = 1.0
            for pair in range(FOX_HEADS // 2):
                sel_k[pair, LANES + part * FOX_HEADS + 2 * pair + hh, hh * FOX_KAUG + FOX_HD + part] = 1.0
    t = np.arange(FOX_WIDTH)
    seg = (t[:, None] // FOX_HD == t[None, :] // FOX_HD).astype(np.float32)
    return jnp.asarray(sel_k, BF16), jnp.asarray(sel_q, BF16), jnp.asarray(ones_q), jnp.asarray(seg, BF16)


def _lower_ones(n):
    t = np.arange(n)
    return jnp.asarray((t[None, :] <= t[:, None]).astype(np.float32), BF16)


def _fox_kernel(*refs, tq, tqp, tk, seq, past, tkp):
    (fq_ref, fk_ref, fv_ref, lf_ref, qg_ref, kg_ref, seg_ref, selk_ref, selq_ref, onesq_ref, tril_ref) = refs[:11]
    refs = refs[11:]
    if past:
        pk_ref, pv_ref, plf_ref, trilp_ref = refs[:4]
        refs = refs[4:]
    refs = refs[1:]
    o_ref, krow_ref, kaug, vt, s_a, s_b, m_s, l_s, acc_s = refs[:9]
    if past:
        pkaug, pvt = refs[9:]
    i = pl.program_id(1)
    blk = min(tk, seq)

    def stage_block(k_bf, v_f32, logf, carry, tril, kaug_dst, vt_dst):
        n = logf.shape[0]
        logf128 = jnp.concatenate([logf, jnp.zeros((n, LANES - FOX_HEADS), F32)], axis=1)
        c = sum(_dot(tril, part) for part in _split3(logf128)) + carry
        hi, mid, lo = (part.astype(F32) for part in _split3(c * (-LOG2E)))
        packed = (hi + pltpu.roll(mid, FOX_HEADS, 1) + pltpu.roll(lo, 2 * FOX_HEADS, 1)).astype(BF16)
        for pair in range(FOX_HEADS // 2):
            x = jnp.concatenate([k_bf[:, pair * LANES:(pair + 1) * LANES], packed], axis=1)
            kaug_dst[0:n, pair * 2 * FOX_KAUG:(pair + 1) * 2 * FOX_KAUG] = _dot(x, selk_ref[pair]).astype(BF16)
        vt_dst[:, 0:n] = v_f32.T.astype(BF16)
        return c[n - 1:n, :]

    @pl.when(i == 0)
    def _prologue():
        carry = jnp.zeros((1, LANES), F32)
        if past:
            for jb in range(past // tkp):
                r = slice(jb * tkp, (jb + 1) * tkp)
                carry = stage_block(pk_ref[0, r, :].astype(BF16), pv_ref[0, r, :], plf_ref[0, r, :],
                                    carry, trilp_ref[...], pkaug.at[jb], pvt.at[jb])
        if blk < tk:
            kaug[...] = jnp.zeros_like(kaug)
            vt[...] = jnp.zeros_like(vt)
        for jb in range(max(seq // tk, 1)):
            r = slice(jb * tk, jb * tk + blk)
            kf = fk_ref[0, r, :]
            kn = kf * _segment_rms_scale(kf, seg_ref, FOX_HD) * kg_ref[...]
            krow_ref[0, r, :] = kn
            carry = stage_block(kn.astype(BF16), fv_ref[0, r, :], lf_ref[0, r, :],
                                carry, tril_ref[...], kaug.at[jb], vt.at[jb])

    qf = fq_ref[0]
    qn = (qf * _segment_rms_scale(qf, seg_ref, FOX_HD) * qg_ref[...] * (FOX_HD ** -0.5 * LOG2E)).astype(BF16)
    if tqp > tq:
        qn = jnp.concatenate([qn, jnp.zeros((tqp - tq, FOX_WIDTH), BF16)], axis=0)
    qt_all = jnp.concatenate(
        [(_dot_nt(selq_ref[...], qn[:, pair * LANES:(pair + 1) * LANES]) + onesq_ref[...]).astype(BF16)
         for pair in range(FOX_HEADS // 2)], axis=0)
    key_i = lax.broadcasted_iota(jnp.int32, (tk, tqp), 0)
    qry_i = lax.broadcasted_iota(jnp.int32, (tk, tqp), 1)
    diag_ok = key_i <= qry_i

    klanes = [slice(h * FOX_KAUG, (h + 1) * FOX_KAUG) for h in range(FOX_HEADS)]
    vrows = [slice(h * FOX_HD, (h + 1) * FOX_HD) for h in range(FOX_HEADS)]
    qts = [qt_all[klanes[h], :] for h in range(FOX_HEADS)]

    def softmax_pv(h, s, v_t, mask):
        if mask is not None:
            s = jnp.where(mask, s, NEG_BIG)
        m = m_s[h]
        m_new = jnp.maximum(m, jnp.max(s, axis=0, keepdims=True))
        p = jnp.exp2(s - m_new)
        alpha = jnp.exp2(m - m_new)
        m_s[h] = m_new
        l_s[h] = alpha * l_s[h] + jnp.sum(p, axis=0, keepdims=True)
        acc_s[h] = alpha * acc_s[h] + _dot(v_t, p.astype(BF16))

    def block_step(j, src, dst):
        for h in range(FOX_HEADS):
            dst[h, 0:tk, :] = _dot(kaug[j + 1, :, klanes[h]], qts[h])
            softmax_pv(h, src[h, 0:tk, :], vt[j, vrows[h], :], None)

    for h in range(FOX_HEADS):
        m_s[h] = jnp.full((1, tqp), NEG_BIG, F32)
        l_s[h] = jnp.zeros((1, tqp), F32)
        acc_s[h] = jnp.zeros((FOX_HD, tqp), F32)
    if past:
        npb = past // tkp
        for h in range(FOX_HEADS):
            s_a[h, 0:tkp, :] = _dot(pkaug[0, :, klanes[h]], qts[h])
        src, dst = s_a, s_b
        for jb in range(npb):
            for h in range(FOX_HEADS):
                if jb + 1 < npb:
                    dst[h, 0:tkp, :] = _dot(pkaug[jb + 1, :, klanes[h]], qts[h])
                else:
                    dst[h, 0:tk, :] = _dot(kaug[0, :, klanes[h]], qts[h])
                softmax_pv(h, src[h, 0:tkp, :], pvt[jb, vrows[h], :], None)
            src, dst = dst, src
        if src is s_b:
            s_a[:, 0:tk, :] = s_b[:, 0:tk, :]
    else:
        for h in range(FOX_HEADS):
            s_a[h, 0:tk, :] = _dot(kaug[0, :, klanes[h]], qts[h])

    def body(jj, carry):
        block_step(2 * jj, s_a, s_b)
        block_step(2 * jj + 1, s_b, s_a)
        return carry

    lax.fori_loop(0, i // 2, body, 0)

    @pl.when(i % 2 == 1)
    def _odd_tail():
        block_step(i - 1, s_a, s_b)
        s_a[:, 0:tk, :] = s_b[:, 0:tk, :]

    outs = []
    for h in range(FOX_HEADS):
        softmax_pv(h, s_a[h, 0:tk, :], vt[i, vrows[h], :], diag_ok)
        outs.append(acc_s[h] / l_s[h])
    o_t = jnp.concatenate(outs, axis=0)
    o_ref[0] = o_t.T[0:tq, :]


def _fox(fq, fk, fv, logf, qg, kg, tq, tk, layer, krow_stack, past_k=None, past_v=None, past_logf=None):
    nb, seq, _ = fq.shape
    depth = fv.shape[0]
    nq = seq // tq
    tqp = max(tq, LANES)
    assert tqp == tk and (tq == tk or nq == 1)
    nkb = max(seq // tk, 1)
    past = 0 if past_k is None else past_k.shape[2]
    tkp = min(past, 256) if past else 0
    sel_k, sel_q, ones_q, seg = _fox_constants()
    kern = functools.partial(_fox_kernel, tq=tq, tqp=tqp, tk=tk, seq=seq, past=past, tkp=tkp)
    full = lambda rows, w: pl.BlockSpec((1, rows, w), lambda b, i: (b, 0, 0))
    slab = lambda w: pl.BlockSpec((None, 1, seq, w), lambda b, i: (layer, b, 0, 0))
    consts = [qg, kg, seg, sel_k, sel_q, ones_q, _lower_ones(min(tk, seq))]
    in_specs = [pl.BlockSpec((1, tq, FOX_WIDTH), lambda b, i: (b, i, 0)), full(seq, FOX_WIDTH), slab(FOX_WIDTH),
                slab(FOX_HEADS)] + [_const_spec(c.shape) for c in consts]
    args = [fq, fk, fv, logf] + consts
    scratch = [pltpu.VMEM((nkb, tk, FOX_HEADS * FOX_KAUG), BF16), pltpu.VMEM((nkb, FOX_WIDTH, tk), BF16),
               pltpu.VMEM((FOX_HEADS, max(tk, tkp), tqp), F32), pltpu.VMEM((FOX_HEADS, max(tk, tkp), tqp), F32),
               pltpu.VMEM((FOX_HEADS, 1, tqp), F32), pltpu.VMEM((FOX_HEADS, 1, tqp), F32),
               pltpu.VMEM((FOX_HEADS, FOX_HD, tqp), F32)]
    if past:
        cache = lambda w: pl.BlockSpec((None, 1, past, w), lambda b, i: (layer, b, 0, 0))
        in_specs += [cache(FOX_WIDTH), cache(FOX_WIDTH), cache(FOX_HEADS), _const_spec((tkp, tkp))]
        args += [past_k, past_v, past_logf, _lower_ones(tkp)]
        scratch += [pltpu.VMEM((past // tkp, tkp, FOX_HEADS * FOX_KAUG), BF16),
                    pltpu.VMEM((past // tkp, FOX_WIDTH, tkp), BF16)]
    aliases = {len(args): 1}
    in_specs += [pl.BlockSpec(memory_space=pl.ANY)]
    args += [krow_stack]
    return pl.pallas_call(
        kern,
        grid=(nb, nq),
        in_specs=in_specs,
        out_specs=[pl.BlockSpec((1, tq, FOX_WIDTH), lambda b, i: (b, i, 0)), slab(FOX_WIDTH)],
        out_shape=[jax.ShapeDtypeStruct((nb, seq, FOX_WIDTH), F32),
                   jax.ShapeDtypeStruct((depth, nb, seq, FOX_WIDTH), F32)],
        input_output_aliases=aliases,
        scratch_shapes=scratch,
        compiler_params=_cparams(("parallel", "arbitrary")),
        name="fox",
    )(*args)


def _out_mlp_kernel(x_ref, s5_ref, gla_ref, fox_ref, wo_ref, g2_ref, wu_ref, wd_ref, y_ref, up_s):
    x1 = (x_ref[...]
          + _dot(s5_ref[...].astype(BF16), wo_ref[0:S5_WIDTH, :])
          + _dot(gla_ref[...].astype(BF16), wo_ref[S5_WIDTH:S5_WIDTH + GLA_WIDTH, :])
          + _dot(fox_ref[...].astype(BF16), wo_ref[S5_WIDTH + GLA_WIDTH:, :]))
    ms = jnp.mean(x1 * x1, axis=-1, keepdims=True)
    h2 = (x1 * lax.rsqrt(ms + EPS) * g2_ref[...]).astype(BF16)
    for c in range(D_FF // FF_CHUNK):
        cols = slice(c * FF_CHUNK, (c + 1) * FF_CHUNK)
        up = jnp.maximum(_dot(h2, wu_ref[:, cols]), 0.0)
        up_s[:, cols] = (up * up).astype(BF16)
    y_ref[...] = x1 + _dot(up_s[...], wd_ref[...])


def _out_mlp(x2d, s5, gla, fox, wo, g2, wu, wd, tm):
    n = x2d.shape[0]
    row = lambda w: pl.BlockSpec((tm, w), lambda i: (i, 0))
    resident = pl.BlockSpec(memory_space=pltpu.VMEM)
    return pl.pallas_call(
        _out_mlp_kernel,
        grid=(n // tm,),
        in_specs=[row(D_MODEL), row(S5_WIDTH), row(GLA_WIDTH), row(FOX_WIDTH),
                  resident, _const_spec((1, D_MODEL)), resident, resident],
        out_specs=row(D_MODEL),
        out_shape=jax.ShapeDtypeStruct((n, D_MODEL), F32),
        scratch_shapes=[pltpu.VMEM((tm, D_FF), BF16)],
        compiler_params=_cparams(("parallel",)),
        name="out_mlp",
    )(x2d, s5, gla, fox, wo, g2, wu, wd)


def _pack_params(norm1_g, w_in, s5_a_re, s5_a_im, s5_log_dt, s5_b_re, s5_b_im, s5_c_re, s5_c_im, s5_d,
                 s5_glu_w, s5_glu_b, gla_a2, gla_a_bias, gla_norm_g, fox_q_norm_g, fox_k_norm_g, fox_f_bias,
                 w_out, norm2_g, w_up, w_down):
    depth = w_in.shape[0]
    o_ga = 1024
    o_fq = o_ga + GLA_RANK
    o_ff = o_fq + 3 * FOX_WIDTH
    w_ff = w_in[:, :, o_ff:o_ff + FOX_HEADS]
    pad = jnp.zeros((depth, D_MODEL, LANES - FOX_HEADS - GLA_RANK), F32)
    w_pack = jnp.concatenate([w_in[:, :, :o_ga], w_ff, w_in[:, :, o_ga:o_fq], pad, w_in[:, :, o_fq:o_ff]],
                             axis=-1).astype(BF16)
    a2p = jnp.zeros((depth, LANES, GLA_KW), F32).at[:, MISC_GA:MISC_GA + GLA_RANK, :].set(gla_a2).astype(BF16)

    dt = jnp.exp(s5_log_dt)[..., None]
    mag = jnp.exp(dt * s5_a_re)
    abar_re = mag * jnp.cos(dt * s5_a_im)
    abar_im = mag * jnp.sin(dt * s5_a_im)
    den = s5_a_re * s5_a_re + s5_a_im * s5_a_im
    zr = ((abar_re - 1.0) * s5_a_re + abar_im * s5_a_im) / den
    zi = (abar_im * s5_a_re - (abar_re - 1.0) * s5_a_im) / den
    bb_re = zr[..., None] * s5_b_re - zi[..., None] * s5_b_im
    bb_im = zr[..., None] * s5_b_im + zi[..., None] * s5_b_re
    eye = jnp.eye(S5_GROUPS, dtype=F32)
    to_b = lambda bb: jnp.einsum('lgnp,gh->lgphn', bb, eye).reshape(depth, S5_WIDTH, S5_GN)
    bmat = jnp.concatenate([to_b(bb_re), to_b(bb_im)], axis=-1).astype(BF16)
    to_c = lambda cc: jnp.einsum('lgpn,gh->lgnhp', cc, eye).reshape(depth, S5_GN, S5_WIDTH)
    cmat = jnp.concatenate([to_c(s5_c_re), -to_c(s5_c_im)], axis=1).astype(BF16)

    return dict(
        norm1_g=norm1_g[:, None, :], w_pack=w_pack, fb_row=fox_f_bias[:, None, :],
        abar_re=abar_re.reshape(depth, 1, S5_GN), abar_im=abar_im.reshape(depth, 1, S5_GN),
        bmat=bmat, cmat=cmat, s5_d=s5_d.reshape(depth, 1, S5_WIDTH),
        glu_w=s5_glu_w.astype(BF16), glu_b=s5_glu_b[:, None, :],
        a2p=a2p, ab=gla_a_bias[:, None, :], ng=jnp.tile(gla_norm_g, (1, GLA_HEADS))[:, None, :],
        qg=jnp.tile(fox_q_norm_g, (1, FOX_HEADS))[:, None, :], kg=jnp.tile(fox_k_norm_g, (1, FOX_HEADS))[:, None, :],
        w_out=w_out.astype(BF16), norm2_g=norm2_g[:, None, :], w_up=w_up.astype(BF16), w_down=w_down.astype(BF16),
    )


def _gla_state_to_kernel(s):
    eye = jnp.eye(GLA_HEADS, dtype=s.dtype)
    return jnp.einsum('bhkv,hg->bhvgk', s, eye).reshape(s.shape[0], GLA_WIDTH, GLA_KW)


def _gla_state_from_kernel(st):
    nb = st.shape[0]
    s5d = st.reshape(nb, GLA_HEADS, GLA_DV, GLA_HEADS, GLA_DK)
    diag = jnp.stack([s5d[:, h, :, h, :] for h in range(GLA_HEADS)], axis=1)
    return jnp.swapaxes(diag, 2, 3)


def _tiles(nb, seq):
    n = nb * seq
    fox_tq = min(256, seq)
    return dict(in_rows=min(1024, n), out_rows=min(512, n), s5_tl=min(64, seq), gla_tl=min(1024, seq),
                gla_chunk=min(GLA_CHUNK, seq), fox_tq=fox_tq, fox_tk=max(fox_tq, LANES))


def _layer(x, p, l, depth, stacks, s5_h0r, s5_h0i, gla_s0, past_k, past_v, past_logf):
    nb, seq, _ = x.shape
    n = nb * seq
    t = _tiles(nb, seq)
    s5_tl, gla_tl, gla_chunk, fox_tq, fox_tk = t['s5_tl'], t['gla_tl'], t['gla_chunk'], t['fox_tq'], t['fox_tk']
    x2d = x.reshape(n, D_MODEL)
    u, qk, gv, gg, misc, fq, fk, fv_stack, logf_stack = _in_proj(
        x2d, p['norm1_g'][l], p['w_pack'][l], p['fb_row'][l], t['in_rows'], l, depth,
        stacks[:2])

    u_t = jnp.swapaxes(u.reshape(nb, seq, S5_WIDTH), 0, 1)
    o_s5_t, hr, hi = _s5(u_t, s5_h0r, s5_h0i, p['abar_re'][l], p['abar_im'][l], p['bmat'][l], p['cmat'][l],
                         p['s5_d'][l], p['glu_w'][l], p['glu_b'][l], s5_tl)
    o_s5 = jnp.swapaxes(o_s5_t, 0, 1).reshape(n, S5_WIDTH)

    b3 = lambda a: a.reshape(nb, seq, a.shape[-1])
    o_gla, s_t = _gla(b3(qk), b3(gv), b3(gg), b3(misc), p['a2p'][l], p['ab'][l], p['ng'][l],
                      _gla_state_to_kernel(gla_s0), gla_tl, gla_chunk)

    b4 = lambda a: a.reshape(depth, nb, seq, a.shape[-1])
    past = ()
    if past_k is not None:
        plen = past_k.shape[2]
        past = (past_k.reshape(depth, nb, plen, FOX_WIDTH), past_v.reshape(depth, nb, plen, FOX_WIDTH), past_logf)
    o_fox, krow_stack = _fox(b3(fq), b3(fk), b4(fv_stack), b4(logf_stack), p['qg'][l], p['kg'][l], fox_tq, fox_tk,
                             l, b4(stacks[2]), *past)

    y = _out_mlp(x2d, o_s5, o_gla.reshape(n, GLA_WIDTH), o_fox.reshape(n, FOX_WIDTH),
                 p['w_out'][l], p['norm2_g'][l], p['w_up'][l], p['w_down'][l], t['out_rows'])
    states = (hr.reshape(nb, S5_GROUPS, S5_STATE), hi.reshape(nb, S5_GROUPS, S5_STATE), _gla_state_from_kernel(s_t))
    return y.reshape(nb, seq, D_MODEL), states, (fv_stack, logf_stack, krow_stack)


def kernel(x_prompt, x_sample, state_s5_re, state_s5_im, state_gla, cache_fox_k, cache_fox_v, cache_fox_logf, norm1_g, w_in, s5_a_re, s5_a_im, s5_log_dt, s5_b_re, s5_b_im, s5_c_re, s5_c_im, s5_d, s5_glu_w, s5_glu_b, gla_a2, gla_a_bias, gla_norm_g, fox_q_norm_g, fox_k_norm_g, fox_f_bias, w_out, norm2_g, w_up, w_down):
    depth = w_in.shape[0]
    p = _pack_params(norm1_g, w_in, s5_a_re, s5_a_im, s5_log_dt, s5_b_re, s5_b_im, s5_c_re, s5_c_im, s5_d,
                     s5_glu_w, s5_glu_b, gla_a2, gla_a_bias, gla_norm_g, fox_q_norm_g, fox_k_norm_g, fox_f_bias,
                     w_out, norm2_g, w_up, w_down)
    bp, lp, _ = x_prompt.shape
    bs, ls, _ = x_sample.shape

    yp, ys = x_prompt, x_sample
    p_st = [[] for _ in range(3)]
    s_st = [[] for _ in range(3)]
    new_stacks = lambda nb, seq: (jnp.zeros((depth, nb, seq, FOX_WIDTH), F32),
                                  jnp.zeros((depth, nb, seq, FOX_HEADS), F32),
                                  jnp.zeros((depth, nb, seq, FOX_WIDTH), F32))
    p_stacks, s_stacks = new_stacks(bp, lp), new_stacks(bs, ls)
    z_s5 = jnp.zeros((bp, S5_GN), F32)
    z_gla = jnp.zeros((bp, GLA_HEADS, GLA_DK, GLA_DV), F32)
    for l in range(depth):
        yp, stp, p_stacks = _layer(yp, p, l, depth, p_stacks, z_s5, z_s5, z_gla, None, None, None)
        ys, sts, s_stacks = _layer(ys, p, l, depth, s_stacks,
                                   state_s5_re[l].reshape(bs, S5_GN), state_s5_im[l].reshape(bs, S5_GN),
                                   state_gla[l], cache_fox_k, cache_fox_v, cache_fox_logf)
        for j in range(3):
            p_st[j].append(stp[j])
            s_st[j].append(sts[j])

    def fox_states(stacks, nb, seq):
        fv, logf, krow = stacks
        return (krow.reshape(depth, nb, seq, FOX_HEADS, FOX_HD), fv.reshape(depth, nb, seq, FOX_HEADS, FOX_HD),
                logf.reshape(depth, nb, seq, FOX_HEADS))

    return (yp, ys, *[jnp.stack(a) for a in p_st], *fox_states(p_stacks, bp, lp),
            *[jnp.stack(a) for a in s_st], *fox_states(s_stacks, bs, ls))
```

```python
import functools
import math

import jax
import jax.numpy as jnp
import numpy as np
from jax import lax
from jax.experimental import pallas as pl
from jax.experimental.pallas import tpu as pltpu

F32 = jnp.float32
BF16 = jnp.bfloat16

D_MODEL = 1024
EPS = 1e-6
LANES = 128

S5_WIDTH = 256
S5_GROUPS = 16
S5_GROUP = 16
S5_STATE = 64
S5_GN = S5_GROUPS * S5_STATE
S5_SUB_STEPS = 16

GLA_WIDTH = 256
GLA_HEADS = 4
GLA_DK = 32
GLA_DV = 64
GLA_KW = 128
GLA_RANK = 16
GLA_TAU = 16.0
GLA_CHUNK = 64
GLA_CUMSUM_SPAN = 256

FOX_WIDTH = 512
FOX_HD = 64
FOX_HEADS = 8
NEG_BIG = -1e30
LOG2E = math.log2(math.e)

D_FF = 4096
FF_CHUNK = 1024

COL_U = 0
COL_QK = 256
COL_GV = 512
COL_GG = 768
COL_MISC = 1024
COL_FQ = 1152
COL_FK = 1664
COL_FV = 2176
N_PACK = 2688
MISC_FF = 0
MISC_GA = 8

VMEM_LIMIT = 56 * 1024 * 1024


def _cparams(sem):
    return pltpu.CompilerParams(dimension_semantics=sem, vmem_limit_bytes=VMEM_LIMIT)


def _const_spec(shape):
    nd = len(shape)
    return pl.BlockSpec(shape, lambda *_: (0,) * nd)


def _log_sigmoid(x):
    return jnp.minimum(x, 0.0) - jnp.log1p(jnp.exp(-jnp.abs(x)))


def _sigmoid(x):
    return 1.0 / (1.0 + jnp.exp(-x))


def _split2(x):
    hi = x.astype(BF16)
    lo = (x - hi.astype(F32)).astype(BF16)
    return hi, lo


def _split3(x):
    hi = x.astype(BF16)
    r = x - hi.astype(F32)
    mid = r.astype(BF16)
    lo = (r - mid.astype(F32)).astype(BF16)
    return hi, mid, lo


def _dot(a, b):
    return jnp.dot(a, b, preferred_element_type=F32)


def _dot_nt(a, b):
    return lax.dot_general(a, b, (((1,), (1,)), ((), ())), preferred_element_type=F32)


def _dot_tn(a, b):
    return lax.dot_general(a, b, (((0,), (0,)), ((), ())), preferred_element_type=F32)


def _segment_rms_scale(x, seg_ref, width):
    hi, lo = _split2(x * x)
    seg = seg_ref[...]
    ms = (_dot(hi, seg) + _dot(lo, seg)) * (1.0 / width)
    return lax.rsqrt(ms + EPS)


def _in_proj_kernel(x_ref, g_ref, w_ref, fb_row_ref, *refs):
    u_ref, qk_ref, gv_ref, gg_ref, misc_ref, fq_ref, fk_ref, fv_ref, logf_ref = refs[2:]
    x = x_ref[...]
    ms = jnp.mean(x * x, axis=-1, keepdims=True)
    h = (x * lax.rsqrt(ms + EPS) * g_ref[...]).astype(BF16)
    for ref, c0 in ((u_ref, COL_U), (qk_ref, COL_QK), (gv_ref, COL_GV), (gg_ref, COL_GG),
                    (fq_ref, COL_FQ), (fk_ref, COL_FK), (fv_ref, COL_FV)):
        ref[...] = _dot(h, w_ref[:, c0:c0 + ref.shape[-1]]).reshape(ref.shape)
    misc = _dot(h, w_ref[:, COL_MISC:COL_MISC + LANES])
    misc_ref[...] = misc
    logf_ref[...] = _log_sigmoid(misc[:, MISC_FF:MISC_FF + FOX_HEADS] + fb_row_ref[...]).reshape(logf_ref.shape)


def _in_proj(x2d, g, w_pack, fb_row, tm, layer, depth, stacks):
    n = x2d.shape[0]
    _, nb, seq, _ = stacks[0].shape
    assert seq % tm == 0 or tm % seq == 0
    row = lambda w: pl.BlockSpec((tm, w), lambda i: (i, 0))
    if tm <= seq:
        per = seq // tm
        slab = lambda w: pl.BlockSpec((None, 1, tm, w), lambda i: (layer, i // per, i % per, 0))
    else:
        slab = lambda w: pl.BlockSpec((None, tm // seq, seq, w), lambda i: (layer, i, 0, 0))
    widths = (S5_WIDTH, 256, GLA_WIDTH, GLA_WIDTH, LANES, FOX_WIDTH, FOX_WIDTH)
    in_specs = [row(D_MODEL), _const_spec((1, D_MODEL)), _const_spec((D_MODEL, N_PACK)), _const_spec((1, FOX_HEADS))]
    args = [x2d, g, w_pack, fb_row]
    aliases = {len(args): len(widths), len(args) + 1: len(widths) + 1}
    in_specs += [pl.BlockSpec(memory_space=pl.ANY)] * 2
    args += list(stacks)
    return pl.pallas_call(
        _in_proj_kernel,
        grid=(n // tm,),
        in_specs=in_specs,
        out_specs=[row(w) for w in widths] + [slab(FOX_WIDTH), slab(FOX_HEADS)],
        out_shape=[jax.ShapeDtypeStruct((n, w), F32) for w in widths]
                  + [jax.ShapeDtypeStruct((depth, nb, seq, FOX_WIDTH), F32),
                     jax.ShapeDtypeStruct((depth, nb, seq, FOX_HEADS), F32)],
        input_output_aliases=aliases,
        compiler_params=_cparams(("parallel",)),
        name="in_proj",
    )(*args)


def _gelu_tanh(x):
    c = math.sqrt(2.0 / math.pi)
    return x * (0.5 * (1.0 + jnp.tanh(c * (x + 0.044715 * (x * x * x)))))


def _s5_kernel(u_ref, h0r_ref, h0i_ref, ar_ref, ai_ref, bmat_ref, cmat_ref, d_ref, gw_ref, gb_ref,
               o_ref, hr_out_ref, hi_out_ref, hbuf, hr_s, hi_s, *, tl, nb):
    i = pl.program_id(0)

    @pl.when(i == 0)
    def _():
        hr_s[...] = h0r_ref[...]
        hi_s[...] = h0i_ref[...]

    rows = tl * nb
    u = u_ref[...].reshape(rows, S5_WIDTH)
    ub = u.astype(BF16)
    sub = min(tl, S5_SUB_STEPS)
    srows = sub * nb
    re = slice(0, S5_GN)
    im = slice(S5_GN, 2 * S5_GN)

    def input_matmul(s):
        hbuf[s * srows:(s + 1) * srows, :] = _dot(ub[s * srows:(s + 1) * srows], bmat_ref[...])

    input_matmul(0)
    hr, hi = hr_s[...], hi_s[...]
    ys = []
    for s in range(tl // sub):
        if (s + 1) * sub < tl:
            input_matmul(s + 1)
        for t in range(s * sub, (s + 1) * sub):
            r = slice(t * nb, (t + 1) * nb)
            ar, ai = ar_ref[...], ai_ref[...]
            nhr = ar * hr - ai * hi + hbuf[r, re]
            nhi = ar * hi + ai * hr + hbuf[r, im]
            hbuf[r, re] = nhr
            hbuf[r, im] = nhi
            hr, hi = nhr, nhi
        ys.append(_dot(hbuf[s * srows:(s + 1) * srows, :].astype(BF16), cmat_ref[...]))
    hr_s[...] = hr
    hi_s[...] = hi

    y = (ys[0] if len(ys) == 1 else jnp.concatenate(ys, axis=0)) + d_ref[...] * u
    y = _gelu_tanh(y)
    z = _dot(y.astype(BF16), gw_ref[...]) + gb_ref[...]
    o_ref[...] = (y * _sigmoid(z)).reshape(tl, nb, S5_WIDTH)
    hr_out_ref[...] = hr_s[...]
    hi_out_ref[...] = hi_s[...]


def _s5(u_t, h0r, h0i, ar, ai, bmat, cmat, d, gw, gb, tl):
    seq, nb, _ = u_t.shape
    kern = functools.partial(_s5_kernel, tl=tl, nb=nb)
    state = _const_spec((nb, S5_GN))
    return pl.pallas_call(
        kern,
        grid=(seq // tl,),
        in_specs=[pl.BlockSpec((tl, nb, S5_WIDTH), lambda i: (i, 0, 0)), state, state,
                  _const_spec((1, S5_GN)), _const_spec((1, S5_GN)),
                  _const_spec((S5_WIDTH, 2 * S5_GN)), _const_spec((2 * S5_GN, S5_WIDTH)),
                  _const_spec((1, S5_WIDTH)), _const_spec((S5_WIDTH, S5_WIDTH)), _const_spec((1, S5_WIDTH))],
        out_specs=[pl.BlockSpec((tl, nb, S5_WIDTH), lambda i: (i, 0, 0)), state, state],
        out_shape=[jax.ShapeDtypeStruct((seq, nb, S5_WIDTH), F32),
                   jax.ShapeDtypeStruct((nb, S5_GN), F32), jax.ShapeDtypeStruct((nb, S5_GN), F32)],
        scratch_shapes=[pltpu.VMEM((tl * nb, 2 * S5_GN), F32),
                        pltpu.VMEM((nb, S5_GN), F32), pltpu.VMEM((nb, S5_GN), F32)],
        compiler_params=_cparams(("arbitrary",)),
        name="s5",
    )(u_t, h0r, h0i, ar, ai, bmat, cmat, d, gw, gb)


def _gla_kernel(qk_ref, v_ref, g_ref, misc_ref, a2_ref, ab_ref, ng_ref, s0_ref,
                tril_ref, mk_ref, mv_ref, ms_ref, causal_ref, seg_ref,
                o_ref, sfin_ref, s_scr, *, tl, chunk):
    i = pl.program_id(1)

    @pl.when(i == 0)
    def _():
        s_scr[...] = s0_ref[0]

    qk = qk_ref[0]
    q = qk[:, :GLA_KW] * (GLA_DK ** -0.5)
    k = qk[:, GLA_KW:]
    v = v_ref[0]
    la = _log_sigmoid(_dot(misc_ref[0].astype(BF16), a2_ref[...]) + ab_ref[...]) * (1.0 / GLA_TAU)

    tril = tril_ref[...]
    span = tril.shape[0]
    parts = _split3(la)
    b = jnp.concatenate([sum(_dot(tril, part[r0:r0 + span]) for part in parts) for r0 in range(0, tl, span)], axis=0)
    nchunk = tl // chunk
    b3 = b.reshape(nchunk, chunk, GLA_KW)
    b_last = b3[:, chunk - 1:chunk, :]
    b_end = (b_last - b3).reshape(tl, GLA_KW)
    q_dec = (q * jnp.exp(b)).astype(BF16)
    k_inv = k * jnp.exp(-b)
    k_end = (k * jnp.exp(b_end)).astype(BF16)
    e_last = jnp.exp(b_last)

    outs = []
    for c in range(nchunk):
        sl = slice(c * chunk, (c + 1) * chunk)
        qd = q_dec[sl]
        kbd = jnp.where(mk_ref[...] > 0, jnp.concatenate([k_inv[sl]] * GLA_HEADS, axis=0), 0.0).astype(BF16)
        att = jnp.where(causal_ref[...] > 0, _dot_nt(qd, kbd), 0.0)
        vc = v[sl]
        vbd = jnp.where(mv_ref[...] > 0, jnp.concatenate([vc] * GLA_HEADS, axis=0), 0.0).astype(BF16)
        s = s_scr[...]
        outs.append(_dot(att.astype(BF16), vbd) + _dot_nt(qd, s.astype(BF16)))
        upd = jnp.where(ms_ref[...] > 0, _dot_tn(vc.astype(BF16), k_end[sl]), 0.0)
        s_scr[...] = s * e_last[c] + upd
    o = outs[0] if nchunk == 1 else jnp.concatenate(outs, axis=0)

    o = o * _segment_rms_scale(o, seg_ref, GLA_DV) * ng_ref[...]
    g = g_ref[0]
    o_ref[0] = o * (g * _sigmoid(g))
    sfin_ref[0] = s_scr[...]


def _gla_masks(tl, chunk):
    t = np.arange(min(tl, GLA_CUMSUM_SPAN))
    tril = ((t[:, None] // chunk == t[None, :] // chunk) & (t[None, :] <= t[:, None])).astype(np.float32)
    r = np.arange(GLA_HEADS * chunk)
    mk = (r[:, None] // chunk == np.arange(GLA_KW)[None, :] // GLA_DK).astype(np.float32)
    mv = (r[:, None] // chunk == np.arange(GLA_WIDTH)[None, :] // GLA_DV).astype(np.float32)
    ms = (np.arange(GLA_WIDTH)[:, None] // GLA_DV == np.arange(GLA_KW)[None, :] // GLA_DK).astype(np.float32)
    causal = (np.arange(chunk)[:, None] >= (r[None, :] % chunk)).astype(np.float32)
    seg = (np.arange(GLA_WIDTH)[:, None] // GLA_DV == np.arange(GLA_WIDTH)[None, :] // GLA_DV).astype(np.float32)
    return (jnp.asarray(tril, BF16), jnp.asarray(mk), jnp.asarray(mv), jnp.asarray(ms), jnp.asarray(causal),
            jnp.asarray(seg, BF16))


def _gla(qk, v, g, misc, a2p, ab, ng, s0t, tl, chunk):
    nb, seq, _ = qk.shape
    kern = functools.partial(_gla_kernel, tl=tl, chunk=chunk)
    masks = _gla_masks(tl, chunk)
    tok = lambda w: pl.BlockSpec((1, tl, w), lambda b, i: (b, i, 0))
    st = pl.BlockSpec((1, GLA_WIDTH, GLA_KW), lambda b, i: (b, 0, 0))
    return pl.pallas_call(
        kern,
        grid=(nb, seq // tl),
        in_specs=[tok(256), tok(GLA_WIDTH), tok(GLA_WIDTH), tok(LANES),
                  _const_spec((LANES, GLA_KW)), _const_spec((1, GLA_KW)), _const_spec((1, GLA_WIDTH)), st]
                 + [_const_spec(m.shape) for m in masks],
        out_specs=[tok(GLA_WIDTH), st],
        out_shape=[jax.ShapeDtypeStruct((nb, seq, GLA_WIDTH), F32),
                   jax.ShapeDtypeStruct((nb, GLA_WIDTH, GLA_KW), F32)],
        scratch_shapes=[pltpu.VMEM((GLA_WIDTH, GLA_KW), F32)],
        compiler_params=_cparams(("parallel", "arbitrary")),
        name="gla",
    )(qk, v, g, misc, a2p, ab, ng, s0t, *masks)


FOX_KAUG = 2 * FOX_HD


def _fox_constants():
    sel_k = np.zeros((FOX_HEADS // 2, 2 * LANES, 2 * FOX_KAUG), np.float32)
    sel_q = np.zeros((2 * FOX_KAUG, LANES), np.float32)
    ones_q = np.zeros((2 * FOX_KAUG, 1), np.float32)
    for hh in range(2):
        for d in range(FOX_HD):
            sel_k[:, hh * FOX_HD + d, hh * FOX_KAUG + d] = 1.0
            sel_q[hh * FOX_KAUG + d, hh * FOX_HD + d] = 1.0
        for part in range(3):
            ones_q[hh * FOX_KAUG + FOX_HD + part, 0] = 1.0
            for pair in range(FOX_HEADS // 2):
                sel_k[pair, LANES + part * FOX_HEADS + 2 * pair + hh, hh * FOX_KAUG + FOX_HD + part] = 1.0
    t = np.arange(FOX_WIDTH)
    seg = (t[:, None] // FOX_HD == t[None, :] // FOX_HD).astype(np.float32)
    return jnp.asarray(sel_k, BF16), jnp.asarray(sel_q, BF16), jnp.asarray(ones_q), jnp.asarray(seg, BF16)


def _lower_ones(n):
    t = np.arange(n)
    return jnp.asarray((t[None, :] <= t[:, None]).astype(np.float32), BF16)


def _fox_kernel(*refs, tq, tqp, tk, seq, past, tkp):
    (fq_ref, fk_ref, fv_ref, lf_ref, qg_ref, kg_ref, seg_ref, selk_ref, selq_ref, onesq_ref, tril_ref) = refs[:11]
    refs = refs[11:]
    if past:
        pk_ref, pv_ref, plf_ref, trilp_ref = refs[:4]
        refs = refs[4:]
    refs = refs[1:]
    o_ref, krow_ref, kaug, vt, s_a, s_b, m_s, l_s, acc_s = refs[:9]
    if past:
        pkaug, pvt = refs[9:]
    i = pl.program_id(1)
    blk = min(tk, seq)

    def stage_block(k_bf, v_f32, logf, carry, tril, kaug_dst, vt_dst):
        n = logf.shape[0]
        logf128 = jnp.concatenate([logf, jnp.zeros((n, LANES - FOX_HEADS), F32)], axis=1)
        c = sum(_dot(tril, part) for part in _split3(logf128)) + carry
        hi, mid, lo = (part.astype(F32) for part in _split3(c * (-LOG2E)))
        packed = (hi + pltpu.roll(mid, FOX_HEADS, 1) + pltpu.roll(lo, 2 * FOX_HEADS, 1)).astype(BF16)
        for pair in range(FOX_HEADS // 2):
            x = jnp.concatenate([k_bf[:, pair * LANES:(pair + 1) * LANES], packed], axis=1)
            kaug_dst[0:n, pair * 2 * FOX_KAUG:(pair + 1) * 2 * FOX_KAUG] = _dot(x, selk_ref[pair]).astype(BF16)
        vt_dst[:, 0:n] = v_f32.T.astype(BF16)
        return c[n - 1:n, :]

    @pl.when(i == 0)
    def _prologue():
        carry = jnp.zeros((1, LANES), F32)
        if past:
            for jb in range(past // tkp):
                r = slice(jb * tkp, (jb + 1) * tkp)
                carry = stage_block(pk_ref[0, r, :].astype(BF16), pv_ref[0, r, :], plf_ref[0, r, :],
                                    carry, trilp_ref[...], pkaug.at[jb], pvt.at[jb])
        if blk < tk:
            kaug[...] = jnp.zeros_like(kaug)
            vt[...] = jnp.zeros_like(vt)
        for jb in range(max(seq // tk, 1)):
            r = slice(jb * tk, jb * tk + blk)
            kf = fk_ref[0, r, :]
            kn = kf * _segment_rms_scale(kf, seg_ref, FOX_HD) * kg_ref[...]
            krow_ref[0, r, :] = kn
            carry = stage_block(kn.astype(BF16), fv_ref[0, r, :], lf_ref[0, r, :],
                                carry, tril_ref[...], kaug.at[jb], vt.at[jb])

    qf = fq_ref[0]
    qn = (qf * _segment_rms_scale(qf, seg_ref, FOX_HD) * qg_ref[...] * (FOX_HD ** -0.5 * LOG2E)).astype(BF16)
    if tqp > tq:
        qn = jnp.concatenate([qn, jnp.zeros((tqp - tq, FOX_WIDTH), BF16)], axis=0)
    qt_all = jnp.concatenate(
        [(_dot_nt(selq_ref[...], qn[:, pair * LANES:(pair + 1) * LANES]) + onesq_ref[...]).astype(BF16)
         for pair in range(FOX_HEADS // 2)], axis=0)
    key_i = lax.broadcasted_iota(jnp.int32, (tk, tqp), 0)
    qry_i = lax.broadcasted_iota(jnp.int32, (tk, tqp), 1)
    diag_ok = key_i <= qry_i

    klanes = [slice(h * FOX_KAUG, (h + 1) * FOX_KAUG) for h in range(FOX_HEADS)]
    vrows = [slice(h * FOX_HD, (h + 1) * FOX_HD) for h in range(FOX_HEADS)]
    qts = [qt_all[klanes[h], :] for h in range(FOX_HEADS)]

    def softmax_pv(h, s, v_t, mask):
        if mask is not None:
            s = jnp.where(mask, s, NEG_BIG)
        m = m_s[h]
        m_new = jnp.maximum(m, jnp.max(s, axis=0, keepdims=True))
        p = jnp.exp2(s - m_new)
        alpha = jnp.exp2(m - m_new)
        m_s[h] = m_new
        l_s[h] = alpha * l_s[h] + jnp.sum(p, axis=0, keepdims=True)
        acc_s[h] = alpha * acc_s[h] + _dot(v_t, p.astype(BF16))

    def block_step(j, src, dst):
        for h in range(FOX_HEADS):
            dst[h, 0:tk, :] = _dot(kaug[j + 1, :, klanes[h]], qts[h])
            softmax_pv(h, src[h, 0:tk, :], vt[j, vrows[h], :], None)

    for h in range(FOX_HEADS):
        m_s[h] = jnp.full((1, tqp), NEG_BIG, F32)
        l_s[h] = jnp.zeros((1, tqp), F32)
        acc_s[h] = jnp.zeros((FOX_HD, tqp), F32)
    if past:
        npb = past // tkp
        for h in range(FOX_HEADS):
            s_a[h, 0:tkp, :] = _dot(pkaug[0, :, klanes[h]], qts[h])
        src, dst = s_a, s_b
        for jb in range(npb):
            for h in range(FOX_HEADS):
                if jb + 1 < npb:
                    dst[h, 0:tkp, :] = _dot(pkaug[jb + 1, :, klanes[h]], qts[h])
                else:
                    dst[h, 0:tk, :] = _dot(kaug[0, :, klanes[h]], qts[h])
                softmax_pv(h, src[h, 0:tkp, :], pvt[jb, vrows[h], :], None)
            src, dst = dst, src
        if src is s_b:
            s_a[:, 0:tk, :] = s_b[:, 0:tk, :]
    else:
        for h in range(FOX_HEADS):
            s_a[h, 0:tk, :] = _dot(kaug[0, :, klanes[h]], qts[h])

    def body(jj, carry):
        block_step(2 * jj, s_a, s_b)
        block_step(2 * jj + 1, s_b, s_a)
        return carry

    lax.fori_loop(0, i // 2, body, 0)

    @pl.when(i % 2 == 1)
    def _odd_tail():
        block_step(i - 1, s_a, s_b)
        s_a[:, 0:tk, :] = s_b[:, 0:tk, :]

    outs = []
    for h in range(FOX_HEADS):
        softmax_pv(h, s_a[h, 0:tk, :], vt[i, vrows[h], :], diag_ok)
        outs.append(acc_s[h] / l_s[h])
    o_t = jnp.concatenate(outs, axis=0)
    o_ref[0] = o_t.T[0:tq, :]


def _fox(fq, fk, fv, logf, qg, kg, tq, tk, layer, krow_stack, past_k=None, past_v=None, past_logf=None):
    nb, seq, _ = fq.shape
    depth = fv.shape[0]
    nq = seq // tq
    tqp = max(tq, LANES)
    assert tqp == tk and (tq == tk or nq == 1)
    nkb = max(seq // tk, 1)
    past = 0 if past_k is None else past_k.shape[2]
    tkp = min(past, 256) if past else 0
    sel_k, sel_q, ones_q, seg = _fox_constants()
    kern = functools.partial(_fox_kernel, tq=tq, tqp=tqp, tk=tk, seq=seq, past=past, tkp=tkp)
    full = lambda rows, w: pl.BlockSpec((1, rows, w), lambda b, i: (b, 0, 0))
    slab = lambda w: pl.BlockSpec((None, 1, seq, w), lambda b, i: (layer, b, 0, 0))
    consts = [qg, kg, seg, sel_k, sel_q, ones_q, _lower_ones(min(tk, seq))]
    in_specs = [pl.BlockSpec((1, tq, FOX_WIDTH), lambda b, i: (b, i, 0)), full(seq, FOX_WIDTH), slab(FOX_WIDTH),
                slab(FOX_HEADS)] + [_const_spec(c.shape) for c in consts]
    args = [fq, fk, fv, logf] + consts
    scratch = [pltpu.VMEM((nkb, tk, FOX_HEADS * FOX_KAUG), BF16), pltpu.VMEM((nkb, FOX_WIDTH, tk), BF16),
               pltpu.VMEM((FOX_HEADS, max(tk, tkp), tqp), F32), pltpu.VMEM((FOX_HEADS, max(tk, tkp), tqp), F32),
               pltpu.VMEM((FOX_HEADS, 1, tqp), F32), pltpu.VMEM((FOX_HEADS, 1, tqp), F32),
               pltpu.VMEM((FOX_HEADS, FOX_HD, tqp), F32)]
    if past:
        cache = lambda w: pl.BlockSpec((None, 1, past, w), lambda b, i: (layer, b, 0, 0))
        in_specs += [cache(FOX_WIDTH), cache(FOX_WIDTH), cache(FOX_HEADS), _const_spec((tkp, tkp))]
        args += [past_k, past_v, past_logf, _lower_ones(tkp)]
        scratch += [pltpu.VMEM((past // tkp, tkp, FOX_HEADS * FOX_KAUG), BF16),
                    pltpu.VMEM((past // tkp, FOX_WIDTH, tkp), BF16)]
    aliases = {len(args): 1}
    in_specs += [pl.BlockSpec(memory_space=pl.ANY)]
    args += [krow_stack]
    return pl.pallas_call(
        kern,
        grid=(nb, nq),
        in_specs=in_specs,
        out_specs=[pl.BlockSpec((1, tq, FOX_WIDTH), lambda b, i: (b, i, 0)), slab(FOX_WIDTH)],
        out_shape=[jax.ShapeDtypeStruct((nb, seq, FOX_WIDTH), F32),
                   jax.ShapeDtypeStruct((depth, nb, seq, FOX_WIDTH), F32)],
        input_output_aliases=aliases,
        scratch_shapes=scratch,
        compiler_params=_cparams(("parallel", "arbitrary")),
        name="fox",
    )(*args)


def _out_mlp_kernel(x_ref, s5_ref, gla_ref, fox_ref, wo_ref, g2_ref, wu_ref, wd_ref, y_ref, up_s):
    x1 = (x_ref[...]
          + _dot(s5_ref[...].astype(BF16), wo_ref[0:S5_WIDTH, :])
          + _dot(gla_ref[...].astype(BF16), wo_ref[S5_WIDTH:S5_WIDTH + GLA_WIDTH, :])
          + _dot(fox_ref[...].astype(BF16), wo_ref[S5_WIDTH + GLA_WIDTH:, :]))
    ms = jnp.mean(x1 * x1, axis=-1, keepdims=True)
    h2 = (x1 * lax.rsqrt(ms + EPS) * g2_ref[...]).astype(BF16)
    for c in range(D_FF // FF_CHUNK):
        cols = slice(c * FF_CHUNK, (c + 1) * FF_CHUNK)
        up = jnp.maximum(_dot(h2, wu_ref[:, cols]), 0.0)
        up_s[:, cols] = (up * up).astype(BF16)
    y_ref[...] = x1 + _dot(up_s[...], wd_ref[...])


def _out_mlp(x2d, s5, gla, fox, wo, g2, wu, wd, tm):
    n = x2d.shape[0]
    row = lambda w: pl.BlockSpec((tm, w), lambda i: (i, 0))
    resident = pl.BlockSpec(memory_space=pltpu.VMEM)
    return pl.pallas_call(
        _out_mlp_kernel,
        grid=(n // tm,),
        in_specs=[row(D_MODEL), row(S5_WIDTH), row(GLA_WIDTH), row(FOX_WIDTH),
                  resident, _const_spec((1, D_MODEL)), resident, resident],
        out_specs=row(D_MODEL),
        out_shape=jax.ShapeDtypeStruct((n, D_MODEL), F32),
        scratch_shapes=[pltpu.VMEM((tm, D_FF), BF16)],
        compiler_params=_cparams(("parallel",)),
        name="out_mlp",
    )(x2d, s5, gla, fox, wo, g2, wu, wd)


def _pack_params(norm1_g, w_in, s5_a_re, s5_a_im, s5_log_dt, s5_b_re, s5_b_im, s5_c_re, s5_c_im, s5_d,
                 s5_glu_w, s5_glu_b, gla_a2, gla_a_bias, gla_norm_g, fox_q_norm_g, fox_k_norm_g, fox_f_bias,
                 w_out, norm2_g, w_up, w_down):
    depth = w_in.shape[0]
    o_ga = 1024
    o_fq = o_ga + GLA_RANK
    o_ff = o_fq + 3 * FOX_WIDTH
    w_ff = w_in[:, :, o_ff:o_ff + FOX_HEADS]
    pad = jnp.zeros((depth, D_MODEL, LANES - FOX_HEADS - GLA_RANK), F32)
    w_pack = jnp.concatenate([w_in[:, :, :o_ga], w_ff, w_in[:, :, o_ga:o_fq], pad, w_in[:, :, o_fq:o_ff]],
                             axis=-1).astype(BF16)
    a2p = jnp.zeros((depth, LANES, GLA_KW), F32).at[:, MISC_GA:MISC_GA + GLA_RANK, :].set(gla_a2).astype(BF16)

    dt = jnp.exp(s5_log_dt)[..., None]
    mag = jnp.exp(dt * s5_a_re)
    abar_re = mag * jnp.cos(dt * s5_a_im)
    abar_im = mag * jnp.sin(dt * s5_a_im)
    den = s5_a_re * s5_a_re + s5_a_im * s5_a_im
    zr = ((abar_re - 1.0) * s5_a_re + abar_im * s5_a_im) / den
    zi = (abar_im * s5_a_re - (abar_re - 1.0) * s5_a_im) / den
    bb_re = zr[..., None] * s5_b_re - zi[..., None] * s5_b_im
    bb_im = zr[..., None] * s5_b_im + zi[..., None] * s5_b_re
    eye = jnp.eye(S5_GROUPS, dtype=F32)
    to_b = lambda bb: jnp.einsum('lgnp,gh->lgphn', bb, eye).reshape(depth, S5_WIDTH, S5_GN)
    bmat = jnp.concatenate([to_b(bb_re), to_b(bb_im)], axis=-1).astype(BF16)
    to_c = lambda cc: jnp.einsum('lgpn,gh->lgnhp', cc, eye).reshape(depth, S5_GN, S5_WIDTH)
    cmat = jnp.concatenate([to_c(s5_c_re), -to_c(s5_c_im)], axis=1).astype(BF16)

    return dict(
        norm1_g=norm1_g[:, None, :], w_pack=w_pack, fb_row=fox_f_bias[:, None, :],
        abar_re=abar_re.reshape(depth, 1, S5_GN), abar_im=abar_im.reshape(depth, 1, S5_GN),
        bmat=bmat, cmat=cmat, s5_d=s5_d.reshape(depth, 1, S5_WIDTH),
        glu_w=s5_glu_w.astype(BF16), glu_b=s5_glu_b[:, None, :],
        a2p=a2p, ab=gla_a_bias[:, None, :], ng=jnp.tile(gla_norm_g, (1, GLA_HEADS))[:, None, :],
        qg=jnp.tile(fox_q_norm_g, (1, FOX_HEADS))[:, None, :], kg=jnp.tile(fox_k_norm_g, (1, FOX_HEADS))[:, None, :],
        w_out=w_out.astype(BF16), norm2_g=norm2_g[:, None, :], w_up=w_up.astype(BF16), w_down=w_down.astype(BF16),
    )


def _gla_state_to_kernel(s):
    eye = jnp.eye(GLA_HEADS, dtype=s.dtype)
    return jnp.einsum('bhkv,hg->bhvgk', s, eye).reshape(s.shape[0], GLA_WIDTH, GLA_KW)


def _gla_state_from_kernel(st):
    nb = st.shape[0]
    s5d = st.reshape(nb, GLA_HEADS, GLA_DV, GLA_HEADS, GLA_DK)
    diag = jnp.stack([s5d[:, h, :, h, :] for h in range(GLA_HEADS)], axis=1)
    return jnp.swapaxes(diag, 2, 3)


def _tiles(nb, seq):
    n = nb * seq
    fox_tq = min(256, seq)
    return dict(in_rows=min(1024, n), out_rows=min(512, n), s5_tl=min(128, seq), gla_tl=min(2048, seq),
                gla_chunk=min(GLA_CHUNK, seq), fox_tq=fox_tq, fox_tk=max(fox_tq, LANES))


def _layer(x, p, l, depth, stacks, s5_h0r, s5_h0i, gla_s0, past_k, past_v, past_logf):
    nb, seq, _ = x.shape
    n = nb * seq
    t = _tiles(nb, seq)
    s5_tl, gla_tl, gla_chunk, fox_tq, fox_tk = t['s5_tl'], t['gla_tl'], t['gla_chunk'], t['fox_tq'], t['fox_tk']
    x2d = x.reshape(n, D_MODEL)
    u, qk, gv, gg, misc, fq, fk, fv_stack, logf_stack = _in_proj(
        x2d, p['norm1_g'][l], p['w_pack'][l], p['fb_row'][l], t['in_rows'], l, depth,
        stacks[:2])

    u_t = jnp.swapaxes(u.reshape(nb, seq, S5_WIDTH), 0, 1)
    o_s5_t, hr, hi = _s5(u_t, s5_h0r, s5_h0i, p['abar_re'][l], p['abar_im'][l], p['bmat'][l], p['cmat'][l],
                         p['s5_d'][l], p['glu_w'][l], p['glu_b'][l], s5_tl)
    o_s5 = jnp.swapaxes(o_s5_t, 0, 1).reshape(n, S5_WIDTH)

    b3 = lambda a: a.reshape(nb, seq, a.shape[-1])
    o_gla, s_t = _gla(b3(qk), b3(gv), b3(gg), b3(misc), p['a2p'][l], p['ab'][l], p['ng'][l],
                      _gla_state_to_kernel(gla_s0), gla_tl, gla_chunk)

    b4 = lambda a: a.reshape(depth, nb, seq, a.shape[-1])
    past = ()
    if past_k is not None:
        plen = past_k.shape[2]
        past = (past_k.reshape(depth, nb, plen, FOX_WIDTH), past_v.reshape(depth, nb, plen, FOX_WIDTH), past_logf)
    o_fox, krow_stack = _fox(b3(fq), b3(fk), b4(fv_stack), b4(logf_stack), p['qg'][l], p['kg'][l], fox_tq, fox_tk,
                             l, b4(stacks[2]), *past)

    y = _out_mlp(x2d, o_s5, o_gla.reshape(n, GLA_WIDTH), o_fox.reshape(n, FOX_WIDTH),
                 p['w_out'][l], p['norm2_g'][l], p['w_up'][l], p['w_down'][l], t['out_rows'])
    states = (hr.reshape(nb, S5_GROUPS, S5_STATE), hi.reshape(nb, S5_GROUPS, S5_STATE), _gla_state_from_kernel(s_t))
    return y.reshape(nb, seq, D_MODEL), states, (fv_stack, logf_stack, krow_stack)


def kernel(x_prompt, x_sample, state_s5_re, state_s5_im, state_gla, cache_fox_k, cache_fox_v, cache_fox_logf, norm1_g, w_in, s5_a_re, s5_a_im, s5_log_dt, s5_b_re, s5_b_im, s5_c_re, s5_c_im, s5_d, s5_glu_w, s5_glu_b, gla_a2, gla_a_bias, gla_norm_g, fox_q_norm_g, fox_k_norm_g, fox_f_bias, w_out, norm2_g, w_up, w_down):
    depth = w_in.shape[0]
    p = _pack_params(norm1_g, w_in, s5_a_re, s5_a_im, s5_log_dt, s5_b_re, s5_b_im, s5_c_re, s5_c_im, s5_d,
                     s5_glu_w, s5_glu_b, gla_a2, gla_a_bias, gla_norm_g, fox_q_norm_g, fox_k_norm_g, fox_f_bias,
                     w_out, norm2_g, w_up, w_down)
    bp, lp, _ = x_prompt.shape
    bs, ls, _ = x_sample.shape

    yp, ys = x_prompt, x_sample
    p_st = [[] for _ in range(3)]
    s_st = [[] for _ in range(3)]
    new_stacks = lambda nb, seq: (jnp.zeros((depth, nb, seq, FOX_WIDTH), F32),
                                  jnp.zeros((depth, nb, seq, FOX_HEADS), F32),
                                  jnp.zeros((depth, nb, seq, FOX_WIDTH), F32))
    p_stacks, s_stacks = new_stacks(bp, lp), new_stacks(bs, ls)
    z_s5 = jnp.zeros((bp, S5_GN), F32)
    z_gla = jnp.zeros((bp, GLA_HEADS, GLA_DK, GLA_DV), F32)
    for l in range(depth):
        yp, stp, p_stacks = _layer(yp, p, l, depth, p_stacks, z_s5, z_s5, z_gla, None, None, None)
        ys, sts, s_stacks = _layer(ys, p, l, depth, s_stacks,
                                   state_s5_re[l].reshape(bs, S5_GN), state_s5_im[l].reshape(bs, S5_GN),
                                   state_gla[l], cache_fox_k, cache_fox_v, cache_fox_logf)
        for j in range(3):
            p_st[j].append(stp[j])
            s_st[j].append(sts[j])

    def fox_states(stacks, nb, seq):
        fv, logf, krow = stacks
        return (krow.reshape(depth, nb, seq, FOX_HEADS, FOX_HD), fv.reshape(depth, nb, seq, FOX_HEADS, FOX_HD),
                logf.reshape(depth, nb, seq, FOX_HEADS))

    return (yp, ys, *[jnp.stack(a) for a in p_st], *fox_states(p_stacks, bp, lp),
            *[jnp.stack(a) for a in s_st], *fox_states(s_stacks, bs, ls))
```
